```python
import math
import jax
import jax.numpy as jnp
from jax import lax
import numpy as np

D_MODEL = 1024
BATCH = 4
SEQ = 4096
DEPTH = 4
DEC_BATCH = 128
DEC_SEQ = 1
PAST_LEN = 8192
PAGE_SIZE = 128

N_MIXERS = 2
N_A_LAYERS = (DEPTH + 1) // 2
N_B_LAYERS = DEPTH // 2
D_FF = 2816
CHUNK = 128
D_GMLP = 2 * D_MODEL
N_GROUPS_A = 8
GROUP_DIM_A = D_GMLP // N_GROUPS_A
N_HEADS = 16
Q_LORA = 256
KV_LORA = 128
NOPE_DIM = 128
ROPE_DIM = 64
QK_DIM = NOPE_DIM + ROPE_DIM
V_DIM = 128
CACHE_DIM = KV_LORA + ROPE_DIM + N_HEADS
ROPE_THETA = 10000.0
Q_BLOCK = 128
EPS = 1e-6
NEG_INF = -1e30

kernel_name = 'macaron_gmlp_mla_hybrid_step'


def rms_norm(x, g):
    xf = x.astype(jnp.float32)
    y = xf * lax.rsqrt(jnp.mean(xf * xf, axis=-1, keepdims=True) + EPS)
    return (y * g.astype(jnp.float32)).astype(x.dtype)


def swiglu(h, w_gate, w_up, w_down):
    return (jax.nn.silu(h @ w_gate) * (h @ w_up)) @ w_down


def rope(x, pos):
    half = ROPE_DIM // 2
    inv = ROPE_THETA ** (-jnp.arange(half, dtype=jnp.float32) / half)
    ang = pos.astype(jnp.float32)[:, None] * inv[None, :]
    cos = jnp.cos(ang)[None, :, None, :].astype(x.dtype)
    sin = jnp.sin(ang)[None, :, None, :].astype(x.dtype)
    x1, x2 = x[..., :half], x[..., half:]
    return jnp.concatenate([x1 * cos - x2 * sin, x1 * sin + x2 * cos], axis=-1)


def chunk_spatial_mix(v, w_s, b_s):
    bsz, s_len, _ = v.shape
    mask = jnp.tril(jnp.ones((CHUNK, CHUNK), dtype=bool))
    w = jnp.where(mask[None], w_s, 0.0).astype(v.dtype)
    b = b_s.astype(v.dtype)
    if s_len < CHUNK:
        w = w[:, :s_len, :s_len]
        b = b[:, :s_len]
        blk, pad = s_len, 0
    else:
        blk, pad = CHUNK, (-s_len) % CHUNK
    vp = jnp.pad(v, ((0, 0), (0, pad), (0, 0))) if pad else v
    n_chunks = vp.shape[1] // blk
    vc = vp.reshape(bsz, n_chunks, blk, N_GROUPS_A, GROUP_DIM_A)
    s = jnp.einsum('gij,bcjgd->bcigd', w, vc) + b.T[None, None, :, :, None]
    return s.reshape(bsz, n_chunks * blk, D_GMLP)[:, :s_len]


def gmlp_mixer(h, w_in, b_in, v_gain, w_s, b_s, w_out):
    z = jax.nn.gelu(h @ w_in + b_in)
    u, v = z[..., :D_GMLP], z[..., D_GMLP:]
    v = rms_norm(v, v_gain)
    s = chunk_spatial_mix(v, w_s, b_s)
    return (u * s) @ w_out, v


def mla_project(h, pos, w_dq, q_gain, w_uq, w_dkv, kv_gain, w_uk, qk_gain_q, qk_gain_k):
    cq = rms_norm(h @ w_dq, q_gain)
    q = jnp.einsum('bsr,rhd->bshd', cq, w_uq)
    q = jnp.concatenate([q[..., :NOPE_DIM], rope(q[..., NOPE_DIM:], pos)], axis=-1)
    q = rms_norm(q, qk_gain_q) * qk_gain_k.astype(q.dtype)
    q_lat = jnp.einsum('bshn,chn->bshc', q[..., :NOPE_DIM], w_uk)
    q_rope = q[..., NOPE_DIM:]
    ckv = h @ w_dkv
    c = rms_norm(ckv[..., :KV_LORA], kv_gain)
    k_rope = rope(ckv[:, :, None, KV_LORA:], pos)[:, :, 0]
    k_nope = jnp.einsum('bsc,chn->bshn', c, w_uk).astype(jnp.float32)
    ssq = jnp.sum(k_nope * k_nope, axis=-1) + jnp.sum(jnp.square(k_rope.astype(jnp.float32)), axis=-1)[..., None]
    k_scale = lax.rsqrt(ssq / QK_DIM + EPS).astype(h.dtype)
    rows = jnp.concatenate([c, k_rope, k_scale], axis=-1)
    return q_lat, q_rope, rows


def mla_attend(q_lat, q_rope, q_pos, keys, k_pos, w_uv, w_o):
    c = keys[..., :KV_LORA]
    kr = keys[..., KV_LORA:KV_LORA + ROPE_DIM]
    ks = jnp.transpose(keys[..., KV_LORA + ROPE_DIM:], (0, 2, 1)).astype(jnp.float32)
    bsz, s_q = q_lat.shape[:2]
    blk = Q_BLOCK if s_q % Q_BLOCK == 0 else s_q
    n_blk = s_q // blk
    scale = QK_DIM ** -0.5

    def block(args):
        ql, qr, qp = args
        s = (jnp.einsum('bqhc,bkc->bhqk', ql, c, preferred_element_type=jnp.float32)
             + jnp.einsum('bqhr,bkr->bhqk', qr, kr, preferred_element_type=jnp.float32))
        s = s * ks[:, :, None, :] * scale
        s = jnp.where(k_pos[None, None, None, :] <= qp[None, None, :, None], s, NEG_INF)
        p = jax.nn.softmax(s, axis=-1).astype(c.dtype)
        return jnp.einsum('bhqk,bkc->bqhc', p, c)

    ql_b = jnp.swapaxes(q_lat.reshape(bsz, n_blk, blk, N_HEADS, KV_LORA), 0, 1)
    qr_b = jnp.swapaxes(q_rope.reshape(bsz, n_blk, blk, N_HEADS, ROPE_DIM), 0, 1)
    qp_b = q_pos.reshape(n_blk, blk)
    o = lax.map(block, (ql_b, qr_b, qp_b))
    o = jnp.swapaxes(o, 0, 1).reshape(bsz, s_q, N_HEADS, KV_LORA)
    o = jnp.einsum('bshc,chv->bshv', o, w_uv).reshape(bsz, s_q, N_HEADS * V_DIM)
    return o @ w_o


def setup_inputs(seed: int = 0) -> dict:
    key = jax.random.key(seed)
    ks = jax.random.split(key, 32)
    f32 = jnp.float32

    def nrm(k, shape, fan_in):
        return jax.random.normal(k, shape, f32) * (fan_in ** -0.5)

    def gain(k, shape):
        return 1.0 + 0.1 * jax.random.normal(k, shape, f32)

    n_pages = PAST_LEN // PAGE_SIZE
    n_used = DEC_BATCH * n_pages
    n_pool = n_used + n_used // 4
    x_prompt = jax.random.normal(ks[0], (BATCH, SEQ, D_MODEL), f32)
    x_sample = jax.random.normal(ks[1], (DEC_BATCH, DEC_SEQ, D_MODEL), f32)
    z = jax.random.normal(ks[2], (N_B_LAYERS, n_pool, PAGE_SIZE, CACHE_DIM), f32)
    is_scale_col = jnp.arange(CACHE_DIM) >= KV_LORA + ROPE_DIM
    cache_mla = jnp.where(is_scale_col, 1.0 + 0.3 * jnp.tanh(z), z)
    page_table = jax.random.permutation(ks[3], n_pool)[:n_used].reshape(DEC_BATCH, n_pages).astype(jnp.int32)
    return {
        'x_prompt': x_prompt,
        'x_sample': x_sample,
        'cache_mla': cache_mla,
        'page_table': page_table,
        'norm_g': gain(ks[4], (DEPTH, 3, D_MODEL)),
        'ffn_w_gate': nrm(ks[5], (DEPTH, 2, D_MODEL, D_FF), D_MODEL),
        'ffn_w_up': nrm(ks[6], (DEPTH, 2, D_MODEL, D_FF), D_MODEL),
        'ffn_w_down': nrm(ks[7], (DEPTH, 2, D_FF, D_MODEL), D_FF),
        'gmlp_w_in': nrm(ks[8], (N_A_LAYERS, D_MODEL, 2 * D_GMLP), D_MODEL),
        'gmlp_b_in': 0.02 * jax.random.normal(ks[9], (N_A_LAYERS, 2 * D_GMLP), f32),
        'gmlp_v_norm': gain(ks[10], (N_A_LAYERS, D_GMLP)),
        'gmlp_w_s': nrm(ks[11], (N_A_LAYERS, N_GROUPS_A, CHUNK, CHUNK), CHUNK),
        'gmlp_b_s': gain(ks[12], (N_A_LAYERS, N_GROUPS_A, CHUNK)),
        'gmlp_w_out': nrm(ks[13], (N_A_LAYERS, D_GMLP, D_MODEL), D_GMLP),
        'mla_w_dq': nrm(ks[14], (N_B_LAYERS, D_MODEL, Q_LORA), D_MODEL),
        'mla_q_norm': gain(ks[15], (N_B_LAYERS, Q_LORA)),
        'mla_w_uq': nrm(ks[16], (N_B_LAYERS, Q_LORA, N_HEADS, QK_DIM), Q_LORA),
        'mla_w_dkv': nrm(ks[17], (N_B_LAYERS, D_MODEL, KV_LORA + ROPE_DIM), D_MODEL),
        'mla_kv_norm': gain(ks[18], (N_B_LAYERS, KV_LORA)),
        'mla_w_uk': nrm(ks[19], (N_B_LAYERS, KV_LORA, N_HEADS, NOPE_DIM), KV_LORA),
        'mla_w_uv': nrm(ks[20], (N_B_LAYERS, KV_LORA, N_HEADS, V_DIM), KV_LORA),
        'mla_qk_gain_q': gain(ks[21], (N_B_LAYERS, QK_DIM)),
        'mla_qk_gain_k': gain(ks[22], (N_B_LAYERS, QK_DIM)),
        'mla_w_o': nrm(ks[23], (N_B_LAYERS, N_HEADS * V_DIM, D_MODEL), N_HEADS * V_DIM),
    }


def reference(x_prompt, x_sample, cache_mla, page_table, norm_g, ffn_w_gate, ffn_w_up, ffn_w_down,
              gmlp_w_in, gmlp_b_in, gmlp_v_norm, gmlp_w_s, gmlp_b_s, gmlp_w_out,
              mla_w_dq, mla_q_norm, mla_w_uq, mla_w_dkv, mla_kv_norm, mla_w_uk, mla_w_uv,
              mla_qk_gain_q, mla_qk_gain_k, mla_w_o):
    s_prompt = x_prompt.shape[1]
    n_dec, s_dec = x_sample.shape[0], x_sample.shape[1]
    past_len = page_table.shape[1] * cache_mla.shape[2]
    pos_p = jnp.arange(s_prompt, dtype=jnp.int32)
    pos_s = past_len + jnp.arange(s_dec, dtype=jnp.int32)
    kpos_s = jnp.arange(past_len + s_dec, dtype=jnp.int32)

    def half_ffn(x, i, k):
        h = rms_norm(x, norm_g[i, 2 * k])
        return x + 0.5 * swiglu(h, ffn_w_gate[i, k], ffn_w_up[i, k], ffn_w_down[i, k])

    xp, xs = x_prompt, x_sample
    new_mla_p, new_mla_s, new_v_s = [], [], []
    for i in range(DEPTH):
        xp = half_ffn(xp, i, 0)
        xs = half_ffn(xs, i, 0)
        hp = rms_norm(xp, norm_g[i, 1])
        hs = rms_norm(xs, norm_g[i, 1])
        j = i // N_MIXERS
        if i % N_MIXERS == 0:
            a_args = (gmlp_w_in[j], gmlp_b_in[j], gmlp_v_norm[j], gmlp_w_s[j], gmlp_b_s[j], gmlp_w_out[j])
            yp, _ = gmlp_mixer(hp, *a_args)
            ys, v_s = gmlp_mixer(hs, *a_args)
            new_v_s.append(v_s)
        else:
            p_args = (mla_w_dq[j], mla_q_norm[j], mla_w_uq[j], mla_w_dkv[j], mla_kv_norm[j],
                      mla_w_uk[j], mla_qk_gain_q[j], mla_qk_gain_k[j])
            ql_p, qr_p, rows_p = mla_project(hp, pos_p, *p_args)
            yp = mla_attend(ql_p, qr_p, pos_p, rows_p, pos_p, mla_w_uv[j], mla_w_o[j])
            ql_s, qr_s, rows_s = mla_project(hs, pos_s, *p_args)
            past = cache_mla[j][page_table].reshape(n_dec, past_len, CACHE_DIM)
            keys_s = jnp.concatenate([past, rows_s], axis=1)
            ys = mla_attend(ql_s, qr_s, pos_s, keys_s, kpos_s, mla_w_uv[j], mla_w_o[j])
            new_mla_p.append(rows_p)
            new_mla_s.append(rows_s)
        xp = xp + yp
        xs = xs + ys
        xp = half_ffn(xp, i, 1)
        xs = half_ffn(xs, i, 1)

    new_mla_prompt = jnp.stack(new_mla_p)
    new_mla_sample = jnp.stack(new_mla_s)
    new_gmlp_v_sample = jnp.stack(new_v_s)
    return (xp, xs, new_mla_prompt, new_mla_sample, new_gmlp_v_sample)
```

```python
import functools

import jax
import jax.numpy as jnp
from jax import lax
from jax.experimental import pallas as pl
from jax.experimental.pallas import tpu as pltpu

F32 = jnp.float32
BF16 = jnp.bfloat16
EPS = 1e-6
NEG_INF = -1e30

LANE = 128
MXU_N = 256
CHUNK = 128
TOKEN_TILE = 384
KV_BLOCK = 512
VMEM_LIMIT = 56 * 1024 * 1024


def _params(*sem):
    return pltpu.CompilerParams(dimension_semantics=sem, vmem_limit_bytes=VMEM_LIMIT)


def _resident(shape, index_map):
    return pl.BlockSpec(shape, index_map, pipeline_mode=pl.Buffered(1))


def _rms(x, gain):
    return (x * lax.rsqrt(jnp.mean(x * x, axis=-1, keepdims=True) + EPS)) * gain


def _ffn_kernel(x_ref, g_ref, wg_ref, wu_ref, wd_ref, o_ref, h_scr, acc_scr):
    x = x_ref[...]
    h_scr[...] = _rms(x, g_ref[...]).astype(BF16)
    acc_scr[...] = jnp.zeros_like(acc_scr)

    def chunk(c, carry):
        h = h_scr[...]
        gate = jnp.dot(h, wg_ref[c], preferred_element_type=F32)
        up = jnp.dot(h, wu_ref[c], preferred_element_type=F32)
        act = (jax.nn.silu(gate) * up).astype(BF16)
        acc_scr[...] += jnp.dot(act, wd_ref[c], preferred_element_type=F32)
        return carry

    lax.fori_loop(0, wg_ref.shape[0], chunk, 0)
    o_ref[...] = x + 0.5 * acc_scr[...]


def _ffn(x, gains, wg, wu, wd, layer, k):
    t, d = x.shape
    nc, fc = wg.shape[2], wg.shape[4]
    tm = TOKEN_TILE
    return pl.pallas_call(
        _ffn_kernel,
        grid=(t // tm,),
        in_specs=[
            pl.BlockSpec((tm, d), lambda i: (i, 0)),
            pl.BlockSpec((None, 1, d), lambda i: (3 * layer + 2 * k, 0, 0)),
            _resident((None, None, nc, d, fc), lambda i: (layer, k, 0, 0, 0)),
            _resident((None, None, nc, d, fc), lambda i: (layer, k, 0, 0, 0)),
            _resident((None, None, nc, fc, d), lambda i: (layer, k, 0, 0, 0)),
        ],
        out_specs=pl.BlockSpec((tm, d), lambda i: (i, 0)),
        out_shape=jax.ShapeDtypeStruct((t, d), F32),
        scratch_shapes=[pltpu.VMEM((tm, d), BF16), pltpu.VMEM((tm, d), F32)],
        compiler_params=_params("arbitrary"),
        name="ffn_half",
    )(x, gains, wg, wu, wd)


def _gmlp_kernel(x_ref, g_ref, wu_ref, wv_ref, bu_ref, bv_ref, vg_ref, ws_ref, bs_ref, wo_ref,
                 o_ref, vs_ref, h_scr, v_scr, vb_scr, acc_scr, *, n_prompt_chunks):
    i = pl.program_id(0)
    tm = x_ref.shape[0]
    n_groups, _, gd = wu_ref.shape
    n_sub = tm // CHUNK
    x = x_ref[...]
    h_scr[...] = _rms(x, g_ref[...]).astype(BF16)

    ssq = jnp.zeros((tm, 1), F32)
    for g in range(n_groups):
        z = jnp.dot(h_scr[...], wv_ref[g], preferred_element_type=F32) + bv_ref[g]
        v = jax.nn.gelu(z, approximate=True)
        v_scr[:, g * gd:(g + 1) * gd] = v
        ssq = ssq + jnp.sum(v * v, axis=-1, keepdims=True)
    inv = lax.rsqrt(ssq / (n_groups * gd) + EPS)
    for g in range(n_groups):
        vn = (v_scr[:, g * gd:(g + 1) * gd] * inv) * vg_ref[g]
        v_scr[:, g * gd:(g + 1) * gd] = vn
        vb_scr[:, g * gd:(g + 1) * gd] = vn.astype(BF16)

    acc_scr[...] = jnp.zeros_like(acc_scr)
    for g in range(n_groups):
        z = jnp.dot(h_scr[...], wu_ref[g], preferred_element_type=F32) + bu_ref[g]
        u = jax.nn.gelu(z, approximate=True)
        mixed = []
        for c in range(n_sub):
            sel = ((i * n_sub + c) >= n_prompt_chunks).astype(jnp.int32)
            vc = vb_scr[c * CHUNK:(c + 1) * CHUNK, g * gd:(g + 1) * gd]
            s = jnp.dot(ws_ref[sel, g], vc, preferred_element_type=F32) + bs_ref[sel, g]
            mixed.append(s)
        gated = (u * jnp.concatenate(mixed, axis=0)).astype(BF16)
        acc_scr[...] += jnp.dot(gated, wo_ref[g], preferred_element_type=F32)
    o_ref[...] = x + acc_scr[...]

    @pl.when(i == pl.num_programs(0) - 1)
    def _():
        vs_ref[...] = v_scr[tm - vs_ref.shape[0]:, :]


def _gmlp(x, gains, wu, wv, bu, bv, vg, ws, bs, wo, layer, j, n_sample, n_prompt_chunks):
    t, d = x.shape
    ng, _, gd = wu.shape[1:]
    tm = TOKEN_TILE
    kern = functools.partial(_gmlp_kernel, n_prompt_chunks=n_prompt_chunks)
    return pl.pallas_call(
        kern,
        grid=(t // tm,),
        in_specs=[
            pl.BlockSpec((tm, d), lambda i: (i, 0)),
            pl.BlockSpec((None, 1, d), lambda i: (3 * layer + 1, 0, 0)),
            _resident((None, ng, d, gd), lambda i: (j, 0, 0, 0)),
            _resident((None, ng, d, gd), lambda i: (j, 0, 0, 0)),
            _resident((None, ng, 1, gd), lambda i: (j, 0, 0, 0)),
            _resident((None, ng, 1, gd), lambda i: (j, 0, 0, 0)),
            _resident((None, ng, 1, gd), lambda i: (j, 0, 0, 0)),
            _resident((None, 2, ng, CHUNK, CHUNK), lambda i: (j, 0, 0, 0, 0)),
            _resident((None, 2, ng, CHUNK, 1), lambda i: (j, 0, 0, 0, 0)),
            _resident((None, ng, gd, d), lambda i: (j, 0, 0, 0)),
        ],
        out_specs=[
            pl.BlockSpec((tm, d), lambda i: (i, 0)),
            pl.BlockSpec((n_sample, ng * gd), lambda i: (0, 0)),
        ],
        out_shape=[
            jax.ShapeDtypeStruct((t, d), F32),
            jax.ShapeDtypeStruct((n_sample, ng * gd), F32),
        ],
        scratch_shapes=[
            pltpu.VMEM((tm, d), BF16),
            pltpu.VMEM((tm, ng * gd), F32),
            pltpu.VMEM((tm, ng * gd), BF16),
            pltpu.VMEM((tm, d), F32),
        ],
        compiler_params=_params("arbitrary"),
        name="gmlp_mixer",
    )(x, gains, wu, wv, bu, bv, vg, ws, bs, wo)


def _mla_proj_kernel(x_ref, g_ref, cosc_ref, sins_ref, wdq_ref, qg_ref, wuq_ref, wdkv_ref, kvg_ref,
                     wuk_ref, wukt_ref, gn_ref, gr_ref,
                     q_ref, rows_ref, k_ref, kst_ref, *, qk_dim, rope_dim, sm_scale):
    n_heads = wuq_ref.shape[0]
    hb = _rms(x_ref[...], g_ref[...]).astype(BF16)
    cosc = cosc_ref[...]
    sins = sins_ref[...]
    tm = hb.shape[0]

    ckv = jnp.dot(hb, wdkv_ref[...], preferred_element_type=F32)
    c = _rms(ckv[:, :LANE], kvg_ref[...])
    kr = ckv[:, LANE:2 * LANE] * cosc + ckv[:, 2 * LANE:] * sins
    k_nope = jnp.dot(c.astype(BF16), wuk_ref[...], preferred_element_type=F32)
    ssq_r = jnp.sum(kr * kr, axis=-1, keepdims=True)
    lane = lax.broadcasted_iota(jnp.int32, (tm, LANE), 1)
    tail = kr
    for h in range(n_heads):
        kn = k_nope[:, h * LANE:(h + 1) * LANE]
        ssq = jnp.sum(kn * kn, axis=-1, keepdims=True) + ssq_r
        ks = lax.rsqrt(ssq / qk_dim + EPS)
        tail = jnp.where(lane == rope_dim + h, ks, tail)
    rows_ref[:, :LANE] = c
    rows_ref[:, LANE:] = tail[:, :rows_ref.shape[1] - LANE]
    k_ref[:, :LANE] = c.astype(BF16)
    k_ref[:, LANE:] = tail.astype(BF16)
    tail_t = jnp.transpose(tail)
    for cidx in range(tm // CHUNK):
        kst_ref[cidx] = tail_t[rope_dim:rope_dim + n_heads, cidx * CHUNK:(cidx + 1) * CHUNK] * sm_scale

    cq = _rms(jnp.dot(hb, wdq_ref[...], preferred_element_type=F32), qg_ref[...]).astype(BF16)
    gn = gn_ref[...]
    gr = gr_ref[...]
    for h in range(n_heads):
        qh = jnp.dot(cq, wuq_ref[h], preferred_element_type=F32)
        qn = qh[:, :LANE]
        qr = qh[:, LANE:2 * LANE] * cosc + qh[:, 2 * LANE:] * sins
        ssq = jnp.sum(qn * qn + qr * qr, axis=-1, keepdims=True)
        inv = lax.rsqrt(ssq / qk_dim + EPS)
        qn = ((qn * inv) * gn[0:1]) * gn[1:2]
        qr = ((qr * inv) * gr[0:1]) * gr[1:2]
        q_ref[h, :, :LANE] = jnp.dot(qn.astype(BF16), wukt_ref[h], preferred_element_type=F32).astype(BF16)
        q_ref[h, :, LANE:] = qr.astype(BF16)


def _mla_proj(x, gains, cosc, sins, w, layer, j, qk_dim, rope_dim):
    t, d = x.shape
    tm = TOKEN_TILE
    n_heads = w["wuq"].shape[1]
    q_lora = w["wdq"].shape[2]
    cache_dim = LANE + rope_dim + n_heads
    kern = functools.partial(_mla_proj_kernel, qk_dim=qk_dim, rope_dim=rope_dim, sm_scale=qk_dim ** -0.5)
    return pl.pallas_call(
        kern,
        grid=(t // tm,),
        in_specs=[
            pl.BlockSpec((tm, d), lambda i: (i, 0)),
            pl.BlockSpec((None, 1, d), lambda i: (3 * layer + 1, 0, 0)),
            pl.BlockSpec((tm, LANE), lambda i: (i, 0)),
            pl.BlockSpec((tm, LANE), lambda i: (i, 0)),
            _resident((None, d, q_lora), lambda i: (j, 0, 0)),
            _resident((None, 1, q_lora), lambda i: (j, 0, 0)),
            _resident((None, n_heads, q_lora, 3 * LANE), lambda i: (j, 0, 0, 0)),
            _resident((None, d, 3 * LANE), lambda i: (j, 0, 0)),
            _resident((None, 1, LANE), lambda i: (j, 0, 0)),
            _resident((None, LANE, n_heads * LANE), lambda i: (j, 0, 0)),
            _resident((None, n_heads, LANE, LANE), lambda i: (j, 0, 0, 0)),
            _resident((None, 2, LANE), lambda i: (j, 0, 0)),
            _resident((None, 2, LANE), lambda i: (j, 0, 0)),
        ],
        out_specs=[
            pl.BlockSpec((n_heads, tm, 2 * LANE), lambda i: (0, i, 0)),
            pl.BlockSpec((tm, cache_dim), lambda i: (i, 0)),
            pl.BlockSpec((tm, 2 * LANE), lambda i: (i, 0)),
            pl.BlockSpec((tm // CHUNK, n_heads, CHUNK), lambda i: (i, 0, 0)),
        ],
        out_shape=[
            jax.ShapeDtypeStruct((n_heads, t, 2 * LANE), BF16),
            jax.ShapeDtypeStruct((t, cache_dim), F32),
            jax.ShapeDtypeStruct((t, 2 * LANE), BF16),
            jax.ShapeDtypeStruct((t // CHUNK, n_heads, CHUNK), F32),
        ],
        compiler_params=_params("arbitrary"),
        name="mla_proj",
    )(x, gains, cosc, sins, w["wdq"], w["qg"], w["wuq"], w["wdkv"], w["kvg"], w["wuk"], w["wukt"],
      w["gn"], w["gr"])


def _flash_kernel(q_ref, k_ref, kst_ref, os_ref, o_ref, s_scr, p_scr, m_scr, l_scr, acc_scr, *, nq):
    g = pl.program_id(0)
    n_prompt_blocks = pl.num_programs(0) - 1

    @pl.when(g < n_prompt_blocks)
    def _():
        _flash_block(g % nq, q_ref, k_ref, kst_ref, o_ref, s_scr, p_scr, m_scr, l_scr, acc_scr)

    @pl.when(g == n_prompt_blocks)
    def _():
        o_ref[...] = os_ref[...].astype(BF16)


def _flash_block(qi, q_ref, k_ref, kst_ref, o_ref, s_scr, p_scr, m_scr, l_scr, acc_scr):
    n_heads, bq, dk = q_ref.shape
    kb = s_scr.shape[1]
    n_sub = kb // CHUNK
    q2d = q_ref[...].reshape(n_heads * bq, dk)
    m_scr[...] = jnp.full_like(m_scr, NEG_INF)
    l_scr[...] = jnp.zeros_like(l_scr)
    acc_scr[...] = jnp.zeros_like(acc_scr)

    def block(j, masked):
        start = pl.multiple_of(j * kb, kb)
        kblk = k_ref[pl.ds(start, kb), :]
        s_scr[...] = lax.dot_general(q2d, kblk, (((1,), (1,)), ((), ())), preferred_element_type=F32)
        if masked:
            kpos = start + lax.broadcasted_iota(jnp.int32, (bq, kb), 1)
            qpos = qi * bq + lax.broadcasted_iota(jnp.int32, (bq, kb), 0)
            visible = kpos <= qpos
        for h in range(n_heads):
            rows = slice(h * bq, (h + 1) * bq)
            ks = jnp.concatenate([kst_ref[j * n_sub + c, h:h + 1, :] for c in range(n_sub)], axis=1)
            s = s_scr[rows, :] * ks
            if masked:
                s = jnp.where(visible, s, NEG_INF)
            m_prev = m_scr[rows, :]
            m_new = jnp.maximum(m_prev, jnp.max(s, axis=-1, keepdims=True))
            alpha = jnp.exp(m_prev - m_new)
            p = jnp.exp(s - m_new)
            l_scr[rows, :] = alpha * l_scr[rows, :] + jnp.sum(p, axis=-1, keepdims=True)
            m_scr[rows, :] = m_new
            p_scr[rows, :] = p.astype(BF16)
            acc_scr[rows, :] = acc_scr[rows, :] * alpha
        vblk = k_ref[pl.ds(start, kb), :CHUNK]
        acc_scr[...] += jnp.dot(p_scr[...], vblk, preferred_element_type=F32)

    last = (qi * bq) // kb

    def body(j, carry):
        block(j, False)
        return carry

    lax.fori_loop(0, last, body, 0)
    block(last, True)

    for h in range(n_heads):
        rows = slice(h * bq, (h + 1) * bq)
        o_ref[:, h * CHUNK:(h + 1) * CHUNK] = (acc_scr[rows, :] / l_scr[rows, :]).astype(BF16)


def _flash(q, kmat, kst, o_s, batch, seq):
    n_heads, t_total, dk = q.shape
    bq = CHUNK
    nq = seq // bq
    kb = KV_BLOCK
    assert o_s.shape[0] == bq and t_total == batch * seq + bq

    def kv_block(g):
        return jnp.minimum(g // nq, batch - 1)

    return pl.pallas_call(
        functools.partial(_flash_kernel, nq=nq),
        grid=(batch * nq + 1,),
        in_specs=[
            pl.BlockSpec((n_heads, bq, dk), lambda g: (0, g, 0)),
            pl.BlockSpec((seq, dk), lambda g: (kv_block(g), 0)),
            pl.BlockSpec((seq // CHUNK, n_heads, CHUNK), lambda g: (kv_block(g), 0, 0)),
            pl.BlockSpec(o_s.shape, lambda g: (0, 0)),
        ],
        out_specs=pl.BlockSpec((bq, n_heads * CHUNK), lambda g: (g, 0)),
        out_shape=jax.ShapeDtypeStruct((t_total, n_heads * CHUNK), BF16),
        scratch_shapes=[
            pltpu.VMEM((n_heads * bq, kb), F32),
            pltpu.VMEM((n_heads * bq, kb), BF16),
            pltpu.VMEM((n_heads * bq, 1), F32),
            pltpu.VMEM((n_heads * bq, 1), F32),
            pltpu.VMEM((n_heads * bq, CHUNK), F32),
        ],
        compiler_params=_params("arbitrary"),
        name="flash_prompt",
    )(q, kmat, kst, o_s)


def _decode_kernel(pt_ref, q_ref, row_ref, cache_ref, o_ref, kbuf, tail_scr, sem,
                   *, layer_slot, rope_dim, sm_scale):
    b = pl.program_id(0)
    nb = pl.num_programs(0)
    n_pages = pt_ref.shape[1]
    cache_dim = cache_ref.shape[3]
    n_heads = q_ref.shape[0]
    slot = b % 2

    def page_copy(seq, p, s):
        return pltpu.make_async_copy(
            cache_ref.at[layer_slot, pt_ref[seq, p]],
            kbuf.at[s, pl.ds(p * CHUNK, CHUNK)],
            sem.at[s])

    def start_all(seq, s):
        def go(p, carry):
            page_copy(seq, p, s).start()
            return carry
        lax.fori_loop(0, n_pages, go, 0)

    @pl.when(b == 0)
    def _():
        tail_scr[...] = jnp.zeros_like(tail_scr)
        start_all(0, 0)

    @pl.when(b + 1 < nb)
    def _():
        start_all(b + 1, 1 - slot)

    def wait_one(p, carry):
        page_copy(b, p, slot).wait()
        return carry
    lax.fori_loop(0, n_pages, wait_one, 0)

    tail_scr[:, :cache_dim - LANE] = kbuf[slot, :, LANE:]
    tail = tail_scr[...]
    kb16 = jnp.concatenate([kbuf[slot, :, :LANE].astype(BF16), tail.astype(BF16)], axis=1)
    q = q_ref[...]
    s = lax.dot_general(q, kb16, (((1,), (1,)), ((), ())), preferred_element_type=F32)
    tail_t = jnp.transpose(tail)
    s = (s * tail_t[rope_dim:rope_dim + n_heads, :]) * sm_scale

    row = row_ref[...]
    qf = q.astype(F32)
    lane = lax.broadcasted_iota(jnp.int32, qf.shape, 1)
    head = lax.broadcasted_iota(jnp.int32, qf.shape, 0)
    ks_new = jnp.sum(jnp.where(lane == LANE + rope_dim + head, row, 0.0), axis=-1, keepdims=True)
    s_new = (jnp.sum(qf * row, axis=-1, keepdims=True) * ks_new) * sm_scale

    m = jnp.maximum(jnp.max(s, axis=-1, keepdims=True), s_new)
    p = jnp.exp(s - m)
    p_new = jnp.exp(s_new - m)
    denom = jnp.sum(p, axis=-1, keepdims=True) + p_new
    o = jnp.dot(p.astype(BF16), kb16[:, :LANE], preferred_element_type=F32) + p_new * row[:, :LANE]
    o_ref[...] = o / denom


def _decode(page_table, q_s, rows_s, cache, layer_slot, rope_dim, qk_dim):
    n_seq, n_heads, dk = q_s.shape
    past = page_table.shape[1] * cache.shape[2]
    kern = functools.partial(_decode_kernel, layer_slot=layer_slot, rope_dim=rope_dim,
                             sm_scale=qk_dim ** -0.5)
    grid_spec = pltpu.PrefetchScalarGridSpec(
        num_scalar_prefetch=1,
        grid=(n_seq,),
        in_specs=[
            pl.BlockSpec((None, n_heads, dk), lambda b, pt: (b, 0, 0)),
            pl.BlockSpec((None, 1, dk), lambda b, pt: (b, 0, 0)),
            pl.BlockSpec(memory_space=pl.ANY),
        ],
        out_specs=pl.BlockSpec((None, n_heads, LANE), lambda b, pt: (b, 0, 0)),
        scratch_shapes=[
            pltpu.VMEM((2, past, cache.shape[3]), F32),
            pltpu.VMEM((past, LANE), F32),
            pltpu.SemaphoreType.DMA((2,)),
        ],
    )
    return pl.pallas_call(
        kern,
        grid_spec=grid_spec,
        out_shape=jax.ShapeDtypeStruct((n_seq, n_heads, LANE), F32),
        compiler_params=_params("arbitrary"),
        name="decode_attn",
    )(page_table, q_s, rows_s, cache)


def _attn_out_kernel(x_ref, o_ref, wuv_ref, wo_ref, y_ref, ov_scr):
    n_heads = wuv_ref.shape[0]
    for h in range(n_heads):
        cols = slice(h * LANE, (h + 1) * LANE)
        ov_scr[:, cols] = jnp.dot(o_ref[:, cols], wuv_ref[h], preferred_element_type=F32).astype(BF16)
    y_ref[...] = x_ref[...] + jnp.dot(ov_scr[...], wo_ref[...], preferred_element_type=F32)


def _attn_out(x, o_all, wuv, wo, j):
    t, d = x.shape
    tm = TOKEN_TILE
    n_heads = wuv.shape[1]
    od = o_all.shape[1]
    return pl.pallas_call(
        _attn_out_kernel,
        grid=(t // tm,),
        in_specs=[
            pl.BlockSpec((tm, d), lambda i: (i, 0)),
            pl.BlockSpec((tm, od), lambda i: (i, 0)),
            _resident((None, n_heads, LANE, LANE), lambda i: (j, 0, 0, 0)),
            _resident((None, od, d), lambda i: (j, 0, 0)),
        ],
        out_specs=pl.BlockSpec((tm, d), lambda i: (i, 0)),
        out_shape=jax.ShapeDtypeStruct((t, d), F32),
        scratch_shapes=[pltpu.VMEM((tm, od), BF16)],
        compiler_params=_params("arbitrary"),
        name="attn_out",
    )(x, o_all, wuv, wo)


def _rope_tables(pos, rope_dim, theta=10000.0):
    half = rope_dim // 2
    inv = theta ** (-jnp.arange(half, dtype=F32) / half)
    ang = pos.astype(F32)[:, None] * inv[None, :]
    cos, sin = jnp.cos(ang), jnp.sin(ang)
    zeros = jnp.zeros((pos.shape[0], LANE - rope_dim), F32)
    return (jnp.concatenate([cos, cos, zeros], axis=1),
            jnp.concatenate([-sin, sin, zeros], axis=1))


def _rope_cols(w, nope, half):
    lead = w.shape[:-1]
    pad_n = jnp.zeros(lead + (LANE - nope,), w.dtype)
    pad_r = jnp.zeros(lead + (LANE - 2 * half,), w.dtype)
    r1, r2 = w[..., nope:nope + half], w[..., nope + half:]
    return jnp.concatenate([w[..., :nope], pad_n, r1, r2, pad_r, r2, r1, pad_r], axis=-1)


def kernel(x_prompt, x_sample, cache_mla, page_table, norm_g, ffn_w_gate, ffn_w_up, ffn_w_down,
           gmlp_w_in, gmlp_b_in, gmlp_v_norm, gmlp_w_s, gmlp_b_s, gmlp_w_out,
           mla_w_dq, mla_q_norm, mla_w_uq, mla_w_dkv, mla_kv_norm, mla_w_uk, mla_w_uv,
           mla_qk_gain_q, mla_qk_gain_k, mla_w_o):
    batch, seq, d = x_prompt.shape
    n_sample = x_sample.shape[0]
    depth = norm_g.shape[0]
    n_a, n_b = gmlp_w_in.shape[0], mla_w_dq.shape[0]
    d_ff = ffn_w_gate.shape[3]
    d_gmlp = gmlp_w_out.shape[1]
    n_groups = gmlp_w_s.shape[1]
    gd = d_gmlp // n_groups
    n_heads, qk_dim = mla_w_uq.shape[2], mla_w_uq.shape[3]
    kv_lora = mla_kv_norm.shape[1]
    nope = mla_w_uk.shape[3]
    rope_dim = qk_dim - nope
    half = rope_dim // 2
    past = page_table.shape[1] * cache_mla.shape[2]
    n_prompt = batch * seq
    t = n_prompt + n_sample
    assert x_sample.shape[1] == 1 and n_sample == CHUNK and kv_lora == LANE and nope == LANE
    assert t % TOKEN_TILE == 0 and seq % KV_BLOCK == 0 and d_ff % MXU_N == 0

    nc = d_ff // MXU_N
    wg = ffn_w_gate.astype(BF16).reshape(depth, 2, d, nc, MXU_N).transpose(0, 1, 3, 2, 4)
    wu = ffn_w_up.astype(BF16).reshape(depth, 2, d, nc, MXU_N).transpose(0, 1, 3, 2, 4)
    wd = ffn_w_down.astype(BF16).reshape(depth, 2, nc, MXU_N, d)
    gains = norm_g.reshape(depth * 3, 1, d)

    w_in = gmlp_w_in.astype(BF16).reshape(n_a, d, 2, n_groups, gd).transpose(0, 2, 3, 1, 4)
    g_wu, g_wv = w_in[:, 0], w_in[:, 1]
    b_in = gmlp_b_in.reshape(n_a, 2, n_groups, 1, gd)
    g_bu, g_bv = b_in[:, 0], b_in[:, 1]
    g_vg = gmlp_v_norm.reshape(n_a, n_groups, 1, gd)
    tril = jnp.tril(jnp.ones((CHUNK, CHUNK), dtype=bool))
    ws_prompt = jnp.where(tril, gmlp_w_s, 0.0)
    ws_sample = jnp.eye(CHUNK, dtype=F32) * gmlp_w_s[:, :, :1, :1]
    g_ws = jnp.stack([ws_prompt, ws_sample], axis=1).astype(BF16)
    bs_sample = jnp.broadcast_to(gmlp_b_s[:, :, :1], gmlp_b_s.shape)
    g_bs = jnp.stack([gmlp_b_s, bs_sample], axis=1)[..., None]
    g_wo = gmlp_w_out.astype(BF16).reshape(n_a, n_groups, gd, d)

    def gain_pair(a, b):
        pad = jnp.zeros((n_b, LANE - rope_dim), F32)
        gn = jnp.stack([a[:, :nope], b[:, :nope]], axis=1)
        gr = jnp.stack([jnp.concatenate([a[:, nope:], pad], 1), jnp.concatenate([b[:, nope:], pad], 1)], axis=1)
        return gn, gr
    gn, gr = gain_pair(mla_qk_gain_q, mla_qk_gain_k)
    mla = {
        "wdq": mla_w_dq.astype(BF16),
        "qg": mla_q_norm[:, None, :],
        "wuq": _rope_cols(mla_w_uq.transpose(0, 2, 1, 3), nope, half).astype(BF16),
        "wdkv": _rope_cols(mla_w_dkv, kv_lora, half).astype(BF16),
        "kvg": mla_kv_norm[:, None, :],
        "wuk": mla_w_uk.reshape(n_b, kv_lora, n_heads * nope).astype(BF16),
        "wukt": mla_w_uk.transpose(0, 2, 3, 1).astype(BF16),
        "gn": gn,
        "gr": gr,
    }
    m_wuv = mla_w_uv.transpose(0, 2, 1, 3).astype(BF16)
    m_wo = mla_w_o.astype(BF16)

    pos = jnp.concatenate([jnp.tile(jnp.arange(seq, dtype=jnp.int32), batch),
                           jnp.full((n_sample,), past, jnp.int32)])
    cosc, sins = _rope_tables(pos, rope_dim)

    x = jnp.concatenate([x_prompt.reshape(n_prompt, d), x_sample.reshape(n_sample, d)], axis=0)
    new_rows, new_v = [], []
    for i in range(depth):
        x = _ffn(x, gains, wg, wu, wd, i, 0)
        j = i // 2
        if i % 2 == 0:
            x, v_s = _gmlp(x, gains, g_wu, g_wv, g_bu, g_bv, g_vg, g_ws, g_bs, g_wo, i, j,
                           n_sample, n_prompt // CHUNK)
            new_v.append(v_s)
        else:
            q, rows, kmat, kst = _mla_proj(x, gains, cosc, sins, mla, i, j, qk_dim, rope_dim)
            q_s = q[:, n_prompt:, :].transpose(1, 0, 2)
            rows_s = jnp.pad(rows[n_prompt:], ((0, 0), (0, 2 * LANE - rows.shape[1])))[:, None, :]
            o_s = _decode(page_table, q_s, rows_s, cache_mla, j, rope_dim, qk_dim)
            o_all = _flash(q, kmat, kst, o_s.reshape(n_sample, -1), batch, seq)
            x = _attn_out(x, o_all, m_wuv, m_wo, j)
            new_rows.append(rows)
        x = _ffn(x, gains, wg, wu, wd, i, 1)

    rows = jnp.stack(new_rows)
    return (x[:n_prompt].reshape(batch, seq, d),
            x[n_prompt:].reshape(n_sample, 1, d),
            rows[:, :n_prompt].reshape(n_b, batch, seq, -1),
            rows[:, n_prompt:].reshape(n_b, n_sample, 1, -1),
            jnp.stack(new_v).reshape(n_a, n_sample, 1, d_gmlp))
```

```python
import functools

import jax
import jax.numpy as jnp
from jax import lax
from jax.experimental import pallas as pl
from jax.experimental.pallas import tpu as pltpu

F32 = jnp.float32
BF16 = jnp.bfloat16
EPS = 1e-6
NEG_INF = -1e30

LANE = 128
MXU_N = 256
CHUNK = 128
TOKEN_TILE = 384
KV_BLOCK = 512
FLASH_HEAD_GROUP = 4
LOG2_E = 1.4426950408889634
VMEM_LIMIT = 56 * 1024 * 1024


def _params(*sem):
    return pltpu.CompilerParams(dimension_semantics=sem, vmem_limit_bytes=VMEM_LIMIT)


def _resident(shape, index_map):
    return pl.BlockSpec(shape, index_map, pipeline_mode=pl.Buffered(1))


def _rms(x, gain):
    return (x * lax.rsqrt(jnp.mean(x * x, axis=-1, keepdims=True) + EPS)) * gain


def _ffn_kernel(x_ref, g_ref, wg_ref, wu_ref, wd_ref, o_ref, h_scr, acc_scr):
    x = x_ref[...]
    h_scr[...] = _rms(x, g_ref[...]).astype(BF16)
    acc_scr[...] = jnp.zeros_like(acc_scr)

    def chunk(c, carry):
        h = h_scr[...]
        gate = jnp.dot(h, wg_ref[c], preferred_element_type=F32)
        up = jnp.dot(h, wu_ref[c], preferred_element_type=F32)
        act = (jax.nn.silu(gate) * up).astype(BF16)
        acc_scr[...] += jnp.dot(act, wd_ref[c], preferred_element_type=F32)
        return carry

    lax.fori_loop(0, wg_ref.shape[0], chunk, 0, unroll=True)
    o_ref[...] = x + 0.5 * acc_scr[...]


def _ffn(x, gains, wg, wu, wd, layer, k):
    t, d = x.shape
    nc, fc = wg.shape[2], wg.shape[4]
    tm = TOKEN_TILE
    return pl.pallas_call(
        _ffn_kernel,
        grid=(t // tm,),
        in_specs=[
            pl.BlockSpec((tm, d), lambda i: (i, 0)),
            pl.BlockSpec((None, 1, d), lambda i: (3 * layer + 2 * k, 0, 0)),
            _resident((None, None, nc, d, fc), lambda i: (layer, k, 0, 0, 0)),
            _resident((None, None, nc, d, fc), lambda i: (layer, k, 0, 0, 0)),
            _resident((None, None, nc, fc, d), lambda i: (layer, k, 0, 0, 0)),
        ],
        out_specs=pl.BlockSpec((tm, d), lambda i: (i, 0)),
        out_shape=jax.ShapeDtypeStruct((t, d), F32),
        scratch_shapes=[pltpu.VMEM((tm, d), BF16), pltpu.VMEM((tm, d), F32)],
        compiler_params=_params("arbitrary"),
        name="ffn_half",
    )(x, gains, wg, wu, wd)


def _gmlp_kernel(x_ref, g_ref, wu_ref, wv_ref, bu_ref, bv_ref, vg_ref, ws_ref, bs_ref, wo_ref,
                 o_ref, vs_ref, h_scr, v_scr, vb_scr, acc_scr, *, n_prompt_chunks):
    i = pl.program_id(0)
    tm = x_ref.shape[0]
    n_groups, _, gd = wu_ref.shape
    n_sub = tm // CHUNK
    x = x_ref[...]
    h_scr[...] = _rms(x, g_ref[...]).astype(BF16)

    ssq = jnp.zeros((tm, 1), F32)
    for g in range(n_groups):
        z = jnp.dot(h_scr[...], wv_ref[g], preferred_element_type=F32) + bv_ref[g]
        v = jax.nn.gelu(z, approximate=True)
        v_scr[:, g * gd:(g + 1) * gd] = v
        ssq = ssq + jnp.sum(v * v, axis=-1, keepdims=True)
    inv = lax.rsqrt(ssq / (n_groups * gd) + EPS)
    for g in range(n_groups):
        vn = (v_scr[:, g * gd:(g + 1) * gd] * inv) * vg_ref[g]
        v_scr[:, g * gd:(g + 1) * gd] = vn
        vb_scr[:, g * gd:(g + 1) * gd] = vn.astype(BF16)

    acc_scr[...] = jnp.zeros_like(acc_scr)
    for g in range(n_groups):
        z = jnp.dot(h_scr[...], wu_ref[g], preferred_element_type=F32) + bu_ref[g]
        u = jax.nn.gelu(z, approximate=True)
        mixed = []
        for c in range(n_sub):
            sel = ((i * n_sub + c) >= n_prompt_chunks).astype(jnp.int32)
            vc = vb_scr[c * CHUNK:(c + 1) * CHUNK, g * gd:(g + 1) * gd]
            s = jnp.dot(ws_ref[sel, g], vc, preferred_element_type=F32) + bs_ref[sel, g]
            mixed.append(s)
        gated = (u * jnp.concatenate(mixed, axis=0)).astype(BF16)
        acc_scr[...] += jnp.dot(gated, wo_ref[g], preferred_element_type=F32)
    o_ref[...] = x + acc_scr[...]

    @pl.when(i == pl.num_programs(0) - 1)
    def _():
        vs_ref[...] = v_scr[tm - vs_ref.shape[0]:, :]


def _gmlp(x, gains, wu, wv, bu, bv, vg, ws, bs, wo, layer, j, n_sample, n_prompt_chunks):
    t, d = x.shape
    ng, _, gd = wu.shape[1:]
    tm = TOKEN_TILE
    kern = functools.partial(_gmlp_kernel, n_prompt_chunks=n_prompt_chunks)
    return pl.pallas_call(
        kern,
        grid=(t // tm,),
        in_specs=[
            pl.BlockSpec((tm, d), lambda i: (i, 0)),
            pl.BlockSpec((None, 1, d), lambda i: (3 * layer + 1, 0, 0)),
            _resident((None, ng, d, gd), lambda i: (j, 0, 0, 0)),
            _resident((None, ng, d, gd), lambda i: (j, 0, 0, 0)),
            _resident((None, ng, 1, gd), lambda i: (j, 0, 0, 0)),
            _resident((None, ng, 1, gd), lambda i: (j, 0, 0, 0)),
            _resident((None, ng, 1, gd), lambda i: (j, 0, 0, 0)),
            _resident((None, 2, ng, CHUNK, CHUNK), lambda i: (j, 0, 0, 0, 0)),
            _resident((None, 2, ng, CHUNK, 1), lambda i: (j, 0, 0, 0, 0)),
            _resident((None, ng, gd, d), lambda i: (j, 0, 0, 0)),
        ],
        out_specs=[
            pl.BlockSpec((tm, d), lambda i: (i, 0)),
            pl.BlockSpec((n_sample, ng * gd), lambda i: (0, 0)),
        ],
        out_shape=[
            jax.ShapeDtypeStruct((t, d), F32),
            jax.ShapeDtypeStruct((n_sample, ng * gd), F32),
        ],
        scratch_shapes=[
            pltpu.VMEM((tm, d), BF16),
            pltpu.VMEM((tm, ng * gd), F32),
            pltpu.VMEM((tm, ng * gd), BF16),
            pltpu.VMEM((tm, d), F32),
        ],
        compiler_params=_params("arbitrary"),
        name="gmlp_mixer",
    )(x, gains, wu, wv, bu, bv, vg, ws, bs, wo)


def _mla_proj_kernel(x_ref, g_ref, cosc_ref, sins_ref, wdq_ref, qg_ref, wuq_ref, wdkv_ref, kvg_ref,
                     wuk_ref, wukt_ref, gn_ref, gr_ref,
                     q_ref, rows_ref, k_ref, kst_ref, *, qk_dim, rope_dim, kst_scale):
    n_heads = wuq_ref.shape[0]
    hb = _rms(x_ref[...], g_ref[...]).astype(BF16)
    cosc = cosc_ref[...]
    sins = sins_ref[...]
    tm = hb.shape[0]

    ckv = jnp.dot(hb, wdkv_ref[...], preferred_element_type=F32)
    c = _rms(ckv[:, :LANE], kvg_ref[...])
    kr = ckv[:, LANE:2 * LANE] * cosc + ckv[:, 2 * LANE:] * sins
    k_nope = jnp.dot(c.astype(BF16), wuk_ref[...], preferred_element_type=F32)
    ssq_r = jnp.sum(kr * kr, axis=-1, keepdims=True)
    lane = lax.broadcasted_iota(jnp.int32, (tm, LANE), 1)
    tail = kr
    for h in range(n_heads):
        kn = k_nope[:, h * LANE:(h + 1) * LANE]
        ssq = jnp.sum(kn * kn, axis=-1, keepdims=True) + ssq_r
        ks = lax.rsqrt(ssq / qk_dim + EPS)
        tail = jnp.where(lane == rope_dim + h, ks, tail)
    rows_ref[:, :LANE] = c
    rows_ref[:, LANE:] = tail[:, :rows_ref.shape[1] - LANE]
    k_ref[:, :LANE] = c.astype(BF16)
    k_ref[:, LANE:] = tail.astype(BF16)
    tail_t = jnp.transpose(tail)
    for cidx in range(tm // CHUNK):
        kst_ref[cidx] = tail_t[rope_dim:rope_dim + n_heads, cidx * CHUNK:(cidx + 1) * CHUNK] * kst_scale

    cq = _rms(jnp.dot(hb, wdq_ref[...], preferred_element_type=F32), qg_ref[...]).astype(BF16)
    gn = gn_ref[...]
    gr = gr_ref[...]
    for h in range(n_heads):
        qh = jnp.dot(cq, wuq_ref[h], preferred_element_type=F32)
        qn = qh[:, :LANE]
        qr = qh[:, LANE:2 * LANE] * cosc + qh[:, 2 * LANE:] * sins
        ssq = jnp.sum(qn * qn + qr * qr, axis=-1, keepdims=True)
        inv = lax.rsqrt(ssq / qk_dim + EPS)
        qn = ((qn * inv) * gn[0:1]) * gn[1:2]
        qr = ((qr * inv) * gr[0:1]) * gr[1:2]
        q_ref[h, :, :LANE] = jnp.dot(qn.astype(BF16), wukt_ref[h], preferred_element_type=F32).astype(BF16)
        q_ref[h, :, LANE:] = qr.astype(BF16)


def _mla_proj(x, gains, cosc, sins, w, layer, j, qk_dim, rope_dim):
    t, d = x.shape
    tm = TOKEN_TILE
    n_heads = w["wuq"].shape[1]
    q_lora = w["wdq"].shape[2]
    cache_dim = LANE + rope_dim + n_heads
    kern = functools.partial(_mla_proj_kernel, qk_dim=qk_dim, rope_dim=rope_dim,
                             kst_scale=qk_dim ** -0.5 * LOG2_E)
    return pl.pallas_call(
        kern,
        grid=(t // tm,),
        in_specs=[
            pl.BlockSpec((tm, d), lambda i: (i, 0)),
            pl.BlockSpec((None, 1, d), lambda i: (3 * layer + 1, 0, 0)),
            pl.BlockSpec((tm, LANE), lambda i: (i, 0)),
            pl.BlockSpec((tm, LANE), lambda i: (i, 0)),
            _resident((None, d, q_lora), lambda i: (j, 0, 0)),
            _resident((None, 1, q_lora), lambda i: (j, 0, 0)),
            _resident((None, n_heads, q_lora, 3 * LANE), lambda i: (j, 0, 0, 0)),
            _resident((None, d, 3 * LANE), lambda i: (j, 0, 0)),
            _resident((None, 1, LANE), lambda i: (j, 0, 0)),
            _resident((None, LANE, n_heads * LANE), lambda i: (j, 0, 0)),
            _resident((None, n_heads, LANE, LANE), lambda i: (j, 0, 0, 0)),
            _resident((None, 2, LANE), lambda i: (j, 0, 0)),
            _resident((None, 2, LANE), lambda i: (j, 0, 0)),
        ],
        out_specs=[
            pl.BlockSpec((n_heads, tm, 2 * LANE), lambda i: (0, i, 0)),
            pl.BlockSpec((tm, cache_dim), lambda i: (i, 0)),
            pl.BlockSpec((tm, 2 * LANE), lambda i: (i, 0)),
            pl.BlockSpec((tm // CHUNK, n_heads, CHUNK), lambda i: (i, 0, 0)),
        ],
        out_shape=[
            jax.ShapeDtypeStruct((n_heads, t, 2 * LANE), BF16),
            jax.ShapeDtypeStruct((t, cache_dim), F32),
            jax.ShapeDtypeStruct((t, 2 * LANE), BF16),
            jax.ShapeDtypeStruct((t // CHUNK, n_heads, CHUNK), F32),
        ],
        compiler_params=_params("arbitrary"),
        name="mla_proj",
    )(x, gains, cosc, sins, w["wdq"], w["qg"], w["wuq"], w["wdkv"], w["kvg"], w["wuk"], w["wukt"],
      w["gn"], w["gr"])


def _flash_kernel(q_ref, k_ref, kst_ref, os_ref, o_ref, s_scr, p_scr, m_scr, l_scr, acc_scr, *, nq):
    g = pl.program_id(0)
    n_prompt_blocks = pl.num_programs(0) - 1

    @pl.when(g < n_prompt_blocks)
    def _():
        _flash_block(g % nq, q_ref, k_ref, kst_ref, o_ref, s_scr, p_scr, m_scr, l_scr, acc_scr)

    @pl.when(g == n_prompt_blocks)
    def _():
        o_ref[...] = os_ref[...].astype(BF16)


def _flash_block(qi, q_ref, k_ref, kst_ref, o_ref, s_scr, p_scr, m_scr, l_scr, acc_scr):
    n_heads, bq, dk = q_ref.shape
    kb = s_scr.shape[1]
    n_sub = kb // CHUNK
    m_scr[...] = jnp.full_like(m_scr, NEG_INF)
    l_scr[...] = jnp.zeros_like(l_scr)
    acc_scr[...] = jnp.zeros_like(acc_scr)

    def block(j, masked):
        start = pl.multiple_of(j * kb, kb)
        kblk = k_ref[pl.ds(start, kb), :]
        vblk = k_ref[pl.ds(start, kb), :CHUNK]
        if masked:
            kpos = start + lax.broadcasted_iota(jnp.int32, (bq, kb), 1)
            qpos = qi * bq + lax.broadcasted_iota(jnp.int32, (bq, kb), 0)
            visible = kpos <= qpos
        for g0 in range(0, n_heads, FLASH_HEAD_GROUP):
            grp = slice(g0 * bq, (g0 + FLASH_HEAD_GROUP) * bq)
            qg = q_ref[g0:g0 + FLASH_HEAD_GROUP].reshape(FLASH_HEAD_GROUP * bq, dk)
            s_scr[grp, :] = lax.dot_general(qg, kblk, (((1,), (1,)), ((), ())), preferred_element_type=F32)
            for h in range(g0, g0 + FLASH_HEAD_GROUP):
                rows = slice(h * bq, (h + 1) * bq)
                ks = jnp.concatenate([kst_ref[j * n_sub + c, h:h + 1, :] for c in range(n_sub)], axis=1)
                s = s_scr[rows, :] * ks
                if masked:
                    s = jnp.where(visible, s, NEG_INF)
                m_prev = m_scr[rows, :]
                m_new = jnp.maximum(m_prev, jnp.max(s, axis=-1, keepdims=True))
                alpha = jnp.exp2(m_prev - m_new)
                p = jnp.exp2(s - jnp.concatenate([m_new] * n_sub, axis=1))
                l_scr[rows, :] = alpha * l_scr[rows, :] + jnp.sum(p, axis=-1, keepdims=True)
                m_scr[rows, :] = m_new
                p_scr[rows, :] = p.astype(BF16)
                acc_scr[rows, :] = acc_scr[rows, :] * alpha
            acc_scr[grp, :] += jnp.dot(p_scr[grp, :], vblk, preferred_element_type=F32)

    last = (qi * bq) // kb

    def body(j, carry):
        block(j, False)
        return carry

    lax.fori_loop(0, last, body, 0)
    block(last, True)

    for h in range(n_heads):
        rows = slice(h * bq, (h + 1) * bq)
        o_ref[:, h * CHUNK:(h + 1) * CHUNK] = (acc_scr[rows, :] / l_scr[rows, :]).astype(BF16)


def _flash(q, kmat, kst, o_s, batch, seq):
    n_heads, t_total, dk = q.shape
    bq = CHUNK
    nq = seq // bq
    kb = KV_BLOCK
    assert o_s.shape[0] == bq and t_total == batch * seq + bq

    def kv_block(g):
        return jnp.minimum(g // nq, batch - 1)

    return pl.pallas_call(
        functools.partial(_flash_kernel, nq=nq),
        grid=(batch * nq + 1,),
        in_specs=[
            pl.BlockSpec((n_heads, bq, dk), lambda g: (0, g, 0)),
            pl.BlockSpec((seq, dk), lambda g: (kv_block(g), 0)),
            pl.BlockSpec((seq // CHUNK, n_heads, CHUNK), lambda g: (kv_block(g), 0, 0)),
            pl.BlockSpec(o_s.shape, lambda g: (0, 0)),
        ],
        out_specs=pl.BlockSpec((bq, n_heads * CHUNK), lambda g: (g, 0)),
        out_shape=jax.ShapeDtypeStruct((t_total, n_heads * CHUNK), BF16),
        scratch_shapes=[
            pltpu.VMEM((n_heads * bq, kb), F32),
            pltpu.VMEM((n_heads * bq, kb), BF16),
            pltpu.VMEM((n_heads * bq, LANE), F32),
            pltpu.VMEM((n_heads * bq, LANE), F32),
            pltpu.VMEM((n_heads * bq, CHUNK), F32),
        ],
        compiler_params=_params("arbitrary"),
        name="flash_prompt",
    )(q, kmat, kst, o_s)


def _decode_kernel(pt_ref, q_ref, row_ref, cache_ref, o_ref, kbuf, sem, *, layer_slot, rope_dim, sm_scale):
    b = pl.program_id(0)
    nb = pl.num_programs(0)
    n_pages = pt_ref.shape[1]
    cache_dim, page = cache_ref.shape[2:]
    n_heads, dk = q_ref.shape
    qk_dim = LANE + rope_dim
    slot = b % 2

    def page_copy(seq, p, s):
        return pltpu.make_async_copy(
            cache_ref.at[layer_slot, pt_ref[seq, p]],
            kbuf.at[s, pl.ds(0, cache_dim), pl.ds(pl.multiple_of(p * page, page), page)],
            sem.at[s])

    def start_all(seq, s):
        def go(p, carry):
            page_copy(seq, p, s).start()
            return carry
        lax.fori_loop(0, n_pages, go, 0)

    @pl.when(b == 0)
    def _():
        kbuf[:, cache_dim:, :] = jnp.zeros((2, dk - cache_dim, kbuf.shape[2]), F32)
        start_all(0, 0)

    @pl.when(b + 1 < nb)
    def _():
        start_all(b + 1, 1 - slot)

    def wait_one(p, carry):
        page_copy(b, p, slot).wait()
        return carry
    lax.fori_loop(0, n_pages, wait_one, 0)

    kt = kbuf[slot]
    kt16 = kt.astype(BF16)
    q = q_ref[...]
    s = jnp.dot(q, kt16, preferred_element_type=F32)
    s = (s * kt[qk_dim:qk_dim + n_heads, :]) * sm_scale

    row = row_ref[...]
    qf = q.astype(F32)
    lane = lax.broadcasted_iota(jnp.int32, qf.shape, 1)
    head = lax.broadcasted_iota(jnp.int32, qf.shape, 0)
    ks_new = jnp.sum(jnp.where(lane == qk_dim + head, row, 0.0), axis=-1, keepdims=True)
    s_new = (jnp.sum(qf * row, axis=-1, keepdims=True) * ks_new) * sm_scale

    m = jnp.maximum(jnp.max(s, axis=-1, keepdims=True), s_new)
    p = jnp.exp(s - m)
    p_new = jnp.exp(s_new - m)
    denom = jnp.sum(p, axis=-1, keepdims=True) + p_new
    pv = lax.dot_general(p.astype(BF16), kt16[:LANE, :], (((1,), (1,)), ((), ())), preferred_element_type=F32)
    o_ref[...] = (pv + p_new * row[:, :LANE]) / denom


def _decode(page_table, q_s, rows_s, cache_t, layer_slot, rope_dim, qk_dim):
    n_seq, n_heads, dk = q_s.shape
    past = page_table.shape[1] * cache_t.shape[3]
    kern = functools.partial(_decode_kernel, layer_slot=layer_slot, rope_dim=rope_dim,
                             sm_scale=qk_dim ** -0.5)
    grid_spec = pltpu.PrefetchScalarGridSpec(
        num_scalar_prefetch=1,
        grid=(n_seq,),
        in_specs=[
            pl.BlockSpec((None, n_heads, dk), lambda b, pt: (b, 0, 0)),
            pl.BlockSpec((None, 1, dk), lambda b, pt: (b, 0, 0)),
            pl.BlockSpec(memory_space=pl.ANY),
        ],
        out_specs=pl.BlockSpec((None, n_heads, LANE), lambda b, pt: (b, 0, 0)),
        scratch_shapes=[
            pltpu.VMEM((2, dk, past), F32),
            pltpu.SemaphoreType.DMA((2,)),
        ],
    )
    return pl.pallas_call(
        kern,
        grid_spec=grid_spec,
        out_shape=jax.ShapeDtypeStruct((n_seq, n_heads, LANE), F32),
        compiler_params=_params("arbitrary"),
        name="decode_attn",
    )(page_table, q_s, rows_s, cache_t)


def _attn_out_kernel(x_ref, o_ref, wuv_ref, wo_ref, y_ref, ov_scr):
    n_heads = wuv_ref.shape[0]
    for h in range(n_heads):
        cols = slice(h * LANE, (h + 1) * LANE)
        ov_scr[:, cols] = jnp.dot(o_ref[:, cols], wuv_ref[h], preferred_element_type=F32).astype(BF16)
    y_ref[...] = x_ref[...] + jnp.dot(ov_scr[...], wo_ref[...], preferred_element_type=F32)


def _attn_out(x, o_all, wuv, wo, j):
    t, d = x.shape
    tm = TOKEN_TILE
    n_heads = wuv.shape[1]
    od = o_all.shape[1]
    return pl.pallas_call(
        _attn_out_kernel,
        grid=(t // tm,),
        in_specs=[
            pl.BlockSpec((tm, d), lambda i: (i, 0)),
            pl.BlockSpec((tm, od), lambda i: (i, 0)),
            _resident((None, n_heads, LANE, LANE), lambda i: (j, 0, 0, 0)),
            _resident((None, od, d), lambda i: (j, 0, 0)),
        ],
        out_specs=pl.BlockSpec((tm, d), lambda i: (i, 0)),
        out_shape=jax.ShapeDtypeStruct((t, d), F32),
        scratch_shapes=[pltpu.VMEM((tm, od), BF16)],
        compiler_params=_params("arbitrary"),
        name="attn_out",
    )(x, o_all, wuv, wo)


def _rope_tables(pos, rope_dim, theta=10000.0):
    half = rope_dim // 2
    inv = theta ** (-jnp.arange(half, dtype=F32) / half)
    ang = pos.astype(F32)[:, None] * inv[None, :]
    cos, sin = jnp.cos(ang), jnp.sin(ang)
    zeros = jnp.zeros((pos.shape[0], LANE - rope_dim), F32)
    return (jnp.concatenate([cos, cos, zeros], axis=1),
            jnp.concatenate([-sin, sin, zeros], axis=1))


def _rope_cols(w, nope, half):
    lead = w.shape[:-1]
    pad_n = jnp.zeros(lead + (LANE - nope,), w.dtype)
    pad_r = jnp.zeros(lead + (LANE - 2 * half,), w.dtype)
    r1, r2 = w[..., nope:nope + half], w[..., nope + half:]
    return jnp.concatenate([w[..., :nope], pad_n, r1, r2, pad_r, r2, r1, pad_r], axis=-1)


def kernel(x_prompt, x_sample, cache_mla, page_table, norm_g, ffn_w_gate, ffn_w_up, ffn_w_down,
           gmlp_w_in, gmlp_b_in, gmlp_v_norm, gmlp_w_s, gmlp_b_s, gmlp_w_out,
           mla_w_dq, mla_q_norm, mla_w_uq, mla_w_dkv, mla_kv_norm, mla_w_uk, mla_w_uv,
           mla_qk_gain_q, mla_qk_gain_k, mla_w_o):
    batch, seq, d = x_prompt.shape
    n_sample = x_sample.shape[0]
    depth = norm_g.shape[0]
    n_a, n_b = gmlp_w_in.shape[0], mla_w_dq.shape[0]
    d_ff = ffn_w_gate.shape[3]
    d_gmlp = gmlp_w_out.shape[1]
    n_groups = gmlp_w_s.shape[1]
    gd = d_gmlp // n_groups
    n_heads, qk_dim = mla_w_uq.shape[2], mla_w_uq.shape[3]
    kv_lora = mla_kv_norm.shape[1]
    nope = mla_w_uk.shape[3]
    rope_dim = qk_dim - nope
    half = rope_dim // 2
    past = page_table.shape[1] * cache_mla.shape[2]
    n_prompt = batch * seq
    t = n_prompt + n_sample
    assert x_sample.shape[1] == 1 and n_sample == CHUNK and kv_lora == LANE and nope == LANE
    assert t % TOKEN_TILE == 0 and seq % KV_BLOCK == 0 and d_ff % MXU_N == 0

    nc = d_ff // MXU_N
    wg = ffn_w_gate.astype(BF16).reshape(depth, 2, d, nc, MXU_N).transpose(0, 1, 3, 2, 4)
    wu = ffn_w_up.astype(BF16).reshape(depth, 2, d, nc, MXU_N).transpose(0, 1, 3, 2, 4)
    wd = ffn_w_down.astype(BF16).reshape(depth, 2, nc, MXU_N, d)
    gains = norm_g.reshape(depth * 3, 1, d)

    w_in = gmlp_w_in.astype(BF16).reshape(n_a, d, 2, n_groups, gd).transpose(0, 2, 3, 1, 4)
    g_wu, g_wv = w_in[:, 0], w_in[:, 1]
    b_in = gmlp_b_in.reshape(n_a, 2, n_groups, 1, gd)
    g_bu, g_bv = b_in[:, 0], b_in[:, 1]
    g_vg = gmlp_v_norm.reshape(n_a, n_groups, 1, gd)
    tril = jnp.tril(jnp.ones((CHUNK, CHUNK), dtype=bool))
    ws_prompt = jnp.where(tril, gmlp_w_s, 0.0)
    ws_sample = jnp.eye(CHUNK, dtype=F32) * gmlp_w_s[:, :, :1, :1]
    g_ws = jnp.stack([ws_prompt, ws_sample], axis=1).astype(BF16)
    bs_sample = jnp.broadcast_to(gmlp_b_s[:, :, :1], gmlp_b_s.shape)
    g_bs = jnp.stack([gmlp_b_s, bs_sample], axis=1)[..., None]
    g_wo = gmlp_w_out.astype(BF16).reshape(n_a, n_groups, gd, d)

    def gain_pair(a, b):
        pad = jnp.zeros((n_b, LANE - rope_dim), F32)
        gn = jnp.stack([a[:, :nope], b[:, :nope]], axis=1)
        gr = jnp.stack([jnp.concatenate([a[:, nope:], pad], 1), jnp.concatenate([b[:, nope:], pad], 1)], axis=1)
        return gn, gr
    gn, gr = gain_pair(mla_qk_gain_q, mla_qk_gain_k)
    mla = {
        "wdq": mla_w_dq.astype(BF16),
        "qg": mla_q_norm[:, None, :],
        "wuq": _rope_cols(mla_w_uq.transpose(0, 2, 1, 3), nope, half).astype(BF16),
        "wdkv": _rope_cols(mla_w_dkv, kv_lora, half).astype(BF16),
        "kvg": mla_kv_norm[:, None, :],
        "wuk": mla_w_uk.reshape(n_b, kv_lora, n_heads * nope).astype(BF16),
        "wukt": mla_w_uk.transpose(0, 2, 3, 1).astype(BF16),
        "gn": gn,
        "gr": gr,
    }
    m_wuv = mla_w_uv.transpose(0, 2, 1, 3).astype(BF16)
    m_wo = mla_w_o.astype(BF16)

    pos = jnp.concatenate([jnp.tile(jnp.arange(seq, dtype=jnp.int32), batch),
                           jnp.full((n_sample,), past, jnp.int32)])
    cosc, sins = _rope_tables(pos, rope_dim)

    cache_t = jnp.swapaxes(cache_mla, 2, 3)

    x = jnp.concatenate([x_prompt.reshape(n_prompt, d), x_sample.reshape(n_sample, d)], axis=0)
    new_rows, new_v = [], []
    for i in range(depth):
        x = _ffn(x, gains, wg, wu, wd, i, 0)
        j = i // 2
        if i % 2 == 0:
            x, v_s = _gmlp(x, gains, g_wu, g_wv, g_bu, g_bv, g_vg, g_ws, g_bs, g_wo, i, j,
                           n_sample, n_prompt // CHUNK)
            new_v.append(v_s)
        else:
            q, rows, kmat, kst = _mla_proj(x, gains, cosc, sins, mla, i, j, qk_dim, rope_dim)
            q_s = q[:, n_prompt:, :].transpose(1, 0, 2)
            rows_s = jnp.pad(rows[n_prompt:], ((0, 0), (0, 2 * LANE - rows.shape[1])))[:, None, :]
            o_s = _decode(page_table, q_s, rows_s, cache_t, j, rope_dim, qk_dim)
            o_all = _flash(q, kmat, kst, o_s.reshape(n_sample, -1), batch, seq)
            x = _attn_out(x, o_all, m_wuv, m_wo, j)
            new_rows.append(rows)
        x = _ffn(x, gains, wg, wu, wd, i, 1)

    rows = jnp.stack(new_rows)
    return (x[:n_prompt].reshape(batch, seq, d),
            x[n_prompt:].reshape(n_sample, 1, d),
            rows[:, :n_prompt].reshape(n_b, batch, seq, -1),
            rows[:, n_prompt:].reshape(n_b, n_sample, 1, -1),
            jnp.stack(new_v).reshape(n_a, n_sample, 1, d_gmlp))
```

```python
import functools

import jax
import jax.numpy as jnp
from jax import lax
from jax.experimental import pallas as pl
from jax.experimental.pallas import tpu as pltpu

F32 = jnp.float32
BF16 = jnp.bfloat16
EPS = 1e-6
NEG_INF = -1e30

LANE = 128
MXU_N = 256
CHUNK = 128
TOKEN_TILE = 384
KV_BLOCK = 512
FLASH_HEAD_GROUP = 2
DECODE_SEQS = 2
LOG2_E = 1.4426950408889634
VMEM_LIMIT = 56 * 1024 * 1024


def _params(*sem):
    return pltpu.CompilerParams(dimension_semantics=sem, vmem_limit_bytes=VMEM_LIMIT)


def _resident(shape, index_map):
    return pl.BlockSpec(shape, index_map, pipeline_mode=pl.Buffered(1))


def _rms(x, gain):
    return (x * lax.rsqrt(jnp.mean(x * x, axis=-1, keepdims=True) + EPS)) * gain


def _ffn_kernel(x_ref, g_ref, wg_ref, wu_ref, wd_ref, o_ref, h_scr, acc_scr):
    x = x_ref[...]
    h_scr[...] = _rms(x, g_ref[...]).astype(BF16)
    acc_scr[...] = jnp.zeros_like(acc_scr)

    for c in range(0, wg_ref.shape[1], MXU_N):
        cols = slice(c, c + MXU_N)
        h = h_scr[...]
        gate = jnp.dot(h, wg_ref[:, cols], preferred_element_type=F32)
        up = jnp.dot(h, wu_ref[:, cols], preferred_element_type=F32)
        act = (jax.nn.silu(gate) * up).astype(BF16)
        acc_scr[...] += jnp.dot(act, wd_ref[cols, :], preferred_element_type=F32)
    o_ref[...] = x + 0.5 * acc_scr[...]


def _ffn(x, gains, wg, wu, wd, layer, k):
    t, d = x.shape
    d_ff = wg.shape[3]
    tm = TOKEN_TILE
    return pl.pallas_call(
        _ffn_kernel,
        grid=(t // tm,),
        in_specs=[
            pl.BlockSpec((tm, d), lambda i: (i, 0)),
            pl.BlockSpec((None, 1, d), lambda i: (3 * layer + 2 * k, 0, 0)),
            _resident((None, None, d, d_ff), lambda i: (layer, k, 0, 0)),
            _resident((None, None, d, d_ff), lambda i: (layer, k, 0, 0)),
            _resident((None, None, d_ff, d), lambda i: (layer, k, 0, 0)),
        ],
        out_specs=pl.BlockSpec((tm, d), lambda i: (i, 0)),
        out_shape=jax.ShapeDtypeStruct((t, d), F32),
        scratch_shapes=[pltpu.VMEM((tm, d), BF16), pltpu.VMEM((tm, d), F32)],
        compiler_params=_params("arbitrary"),
        name="ffn_half",
    )(x, gains, wg, wu, wd)


def _gmlp_kernel(x_ref, g_ref, wu_ref, wv_ref, bu_ref, bv_ref, vg_ref, ws_ref, bs_ref, wo_ref,
                 o_ref, vs_ref, h_scr, v_scr, vb_scr, acc_scr, *, n_prompt_chunks):
    i = pl.program_id(0)
    tm = x_ref.shape[0]
    n_groups, _, gd = wu_ref.shape
    n_sub = tm // CHUNK
    x = x_ref[...]
    h_scr[...] = _rms(x, g_ref[...]).astype(BF16)

    ssq = jnp.zeros((tm, 1), F32)
    for g in range(n_groups):
        z = jnp.dot(h_scr[...], wv_ref[g], preferred_element_type=F32) + bv_ref[g]
        v = jax.nn.gelu(z, approximate=True)
        v_scr[:, g * gd:(g + 1) * gd] = v
        ssq = ssq + jnp.sum(v * v, axis=-1, keepdims=True)
    inv = lax.rsqrt(ssq / (n_groups * gd) + EPS)
    for g in range(n_groups):
        vn = (v_scr[:, g * gd:(g + 1) * gd] * inv) * vg_ref[g]
        v_scr[:, g * gd:(g + 1) * gd] = vn
        vb_scr[:, g * gd:(g + 1) * gd] = vn.astype(BF16)

    acc_scr[...] = jnp.zeros_like(acc_scr)
    for g in range(n_groups):
        z = jnp.dot(h_scr[...], wu_ref[g], preferred_element_type=F32) + bu_ref[g]
        u = jax.nn.gelu(z, approximate=True)
        mixed = []
        for c in range(n_sub):
            sel = ((i * n_sub + c) >= n_prompt_chunks).astype(jnp.int32)
            vc = vb_scr[c * CHUNK:(c + 1) * CHUNK, g * gd:(g + 1) * gd]
            s = jnp.dot(ws_ref[sel, g], vc, preferred_element_type=F32) + bs_ref[sel, g]
            mixed.append(s)
        gated = (u * jnp.concatenate(mixed, axis=0)).astype(BF16)
        acc_scr[...] += jnp.dot(gated, wo_ref[g], preferred_element_type=F32)
    o_ref[...] = x + acc_scr[...]

    @pl.when(i == pl.num_programs(0) - 1)
    def _():
        vs_ref[...] = v_scr[tm - vs_ref.shape[0]:, :]


def _gmlp(x, gains, wu, wv, bu, bv, vg, ws, bs, wo, layer, j, n_sample, n_prompt_chunks):
    t, d = x.shape
    ng, _, gd = wu.shape[1:]
    tm = TOKEN_TILE
    kern = functools.partial(_gmlp_kernel, n_prompt_chunks=n_prompt_chunks)
    return pl.pallas_call(
        kern,
        grid=(t // tm,),
        in_specs=[
            pl.BlockSpec((tm, d), lambda i: (i, 0)),
            pl.BlockSpec((None, 1, d), lambda i: (3 * layer + 1, 0, 0)),
            _resident((None, ng, d, gd), lambda i: (j, 0, 0, 0)),
            _resident((None, ng, d, gd), lambda i: (j, 0, 0, 0)),
            _resident((None, ng, 1, gd), lambda i: (j, 0, 0, 0)),
            _resident((None, ng, 1, gd), lambda i: (j, 0, 0, 0)),
            _resident((None, ng, 1, gd), lambda i: (j, 0, 0, 0)),
            _resident((None, 2, ng, CHUNK, CHUNK), lambda i: (j, 0, 0, 0, 0)),
            _resident((None, 2, ng, CHUNK, 1), lambda i: (j, 0, 0, 0, 0)),
            _resident((None, ng, gd, d), lambda i: (j, 0, 0, 0)),
        ],
        out_specs=[
            pl.BlockSpec((tm, d), lambda i: (i, 0)),
            pl.BlockSpec((n_sample, ng * gd), lambda i: (0, 0)),
        ],
        out_shape=[
            jax.ShapeDtypeStruct((t, d), F32),
            jax.ShapeDtypeStruct((n_sample, ng * gd), F32),
        ],
        scratch_shapes=[
            pltpu.VMEM((tm, d), BF16),
            pltpu.VMEM((tm, ng * gd), F32),
            pltpu.VMEM((tm, ng * gd), BF16),
            pltpu.VMEM((tm, d), F32),
        ],
        compiler_params=_params("arbitrary"),
        name="gmlp_mixer",
    )(x, gains, wu, wv, bu, bv, vg, ws, bs, wo)


def _mla_proj_kernel(x_ref, g_ref, cosc_ref, sins_ref, wdq_ref, qg_ref, wuq_ref, wdkv_ref, kvg_ref,
                     wuk_ref, wukt_ref, gn_ref, gr_ref,
                     q_ref, rows_ref, k_ref, kst_ref, *, qk_dim, rope_dim, kst_scale):
    n_heads = wuq_ref.shape[0]
    hb = _rms(x_ref[...], g_ref[...]).astype(BF16)
    cosc = cosc_ref[...]
    sins = sins_ref[...]
    tm = hb.shape[0]

    ckv = jnp.dot(hb, wdkv_ref[...], preferred_element_type=F32)
    c = _rms(ckv[:, :LANE], kvg_ref[...])
    kr = ckv[:, LANE:2 * LANE] * cosc + ckv[:, 2 * LANE:] * sins
    k_nope = jnp.dot(c.astype(BF16), wuk_ref[...], preferred_element_type=F32)
    ssq_r = jnp.sum(kr * kr, axis=-1, keepdims=True)
    lane = lax.broadcasted_iota(jnp.int32, (tm, LANE), 1)
    tail = kr
    for h in range(n_heads):
        kn = k_nope[:, h * LANE:(h + 1) * LANE]
        ssq = jnp.sum(kn * kn, axis=-1, keepdims=True) + ssq_r
        ks = lax.rsqrt(ssq / qk_dim + EPS)
        tail = jnp.where(lane == rope_dim + h, ks, tail)
    rows_ref[:, :LANE] = c
    rows_ref[:, LANE:] = tail[:, :rows_ref.shape[1] - LANE]
    k_ref[:, :LANE] = c.astype(BF16)
    k_ref[:, LANE:] = tail.astype(BF16)
    tail_t = jnp.transpose(tail)
    for cidx in range(tm // CHUNK):
        kst_ref[cidx] = tail_t[rope_dim:rope_dim + n_heads, cidx * CHUNK:(cidx + 1) * CHUNK] * kst_scale

    cq = _rms(jnp.dot(hb, wdq_ref[...], preferred_element_type=F32), qg_ref[...]).astype(BF16)
    gn = gn_ref[...]
    gr = gr_ref[...]
    for h in range(n_heads):
        qh = jnp.dot(cq, wuq_ref[h], preferred_element_type=F32)
        qn = qh[:, :LANE]
        qr = qh[:, LANE:2 * LANE] * cosc + qh[:, 2 * LANE:] * sins
        ssq = jnp.sum(qn * qn + qr * qr, axis=-1, keepdims=True)
        inv = lax.rsqrt(ssq / qk_dim + EPS)
        qn = ((qn * inv) * gn[0:1]) * gn[1:2]
        qr = ((qr * inv) * gr[0:1]) * gr[1:2]
        q_ref[h, :, :LANE] = jnp.dot(qn.astype(BF16), wukt_ref[h], preferred_element_type=F32).astype(BF16)
        q_ref[h, :, LANE:] = qr.astype(BF16)


def _mla_proj(x, gains, cosc, sins, w, layer, j, qk_dim, rope_dim):
    t, d = x.shape
    tm = TOKEN_TILE
    n_heads = w["wuq"].shape[1]
    q_lora = w["wdq"].shape[2]
    cache_dim = LANE + rope_dim + n_heads
    kern = functools.partial(_mla_proj_kernel, qk_dim=qk_dim, rope_dim=rope_dim,
                             kst_scale=qk_dim ** -0.5 * LOG2_E)
    return pl.pallas_call(
        kern,
        grid=(t // tm,),
        in_specs=[
            pl.BlockSpec((tm, d), lambda i: (i, 0)),
            pl.BlockSpec((None, 1, d), lambda i: (3 * layer + 1, 0, 0)),
            pl.BlockSpec((tm, LANE), lambda i: (i, 0)),
            pl.BlockSpec((tm, LANE), lambda i: (i, 0)),
            _resident((None, d, q_lora), lambda i: (j, 0, 0)),
            _resident((None, 1, q_lora), lambda i: (j, 0, 0)),
            _resident((None, n_heads, q_lora, 3 * LANE), lambda i: (j, 0, 0, 0)),
            _resident((None, d, 3 * LANE), lambda i: (j, 0, 0)),
            _resident((None, 1, LANE), lambda i: (j, 0, 0)),
            _resident((None, LANE, n_heads * LANE), lambda i: (j, 0, 0)),
            _resident((None, n_heads, LANE, LANE), lambda i: (j, 0, 0, 0)),
            _resident((None, 2, LANE), lambda i: (j, 0, 0)),
            _resident((None, 2, LANE), lambda i: (j, 0, 0)),
        ],
        out_specs=[
            pl.BlockSpec((n_heads, tm, 2 * LANE), lambda i: (0, i, 0)),
            pl.BlockSpec((tm, cache_dim), lambda i: (i, 0)),
            pl.BlockSpec((tm, 2 * LANE), lambda i: (i, 0)),
            pl.BlockSpec((tm // CHUNK, n_heads, CHUNK), lambda i: (i, 0, 0)),
        ],
        out_shape=[
            jax.ShapeDtypeStruct((n_heads, t, 2 * LANE), BF16),
            jax.ShapeDtypeStruct((t, cache_dim), F32),
            jax.ShapeDtypeStruct((t, 2 * LANE), BF16),
            jax.ShapeDtypeStruct((t // CHUNK, n_heads, CHUNK), F32),
        ],
        compiler_params=_params("arbitrary"),
        name="mla_proj",
    )(x, gains, cosc, sins, w["wdq"], w["qg"], w["wuq"], w["wdkv"], w["kvg"], w["wuk"], w["wukt"],
      w["gn"], w["gr"])


def _flash_kernel(q_ref, k_ref, kst_ref, os_ref, o_ref, s_scr, p_scr, m_scr, l_scr, acc_scr, *, nq):
    g = pl.program_id(0)
    n_prompt_blocks = pl.num_programs(0) - 1

    @pl.when(g < n_prompt_blocks)
    def _():
        _flash_block(g % nq, q_ref, k_ref, kst_ref, o_ref, s_scr, p_scr, m_scr, l_scr, acc_scr)

    @pl.when(g == n_prompt_blocks)
    def _():
        o_ref[...] = os_ref[...].astype(BF16)


def _flash_block(qi, q_ref, k_ref, kst_ref, o_ref, s_scr, p_scr, m_scr, l_scr, acc_scr):
    n_heads, bq, dk = q_ref.shape
    kb = s_scr.shape[1]
    n_sub = kb // CHUNK
    m_scr[...] = jnp.full_like(m_scr, NEG_INF)
    l_scr[...] = jnp.zeros_like(l_scr)
    acc_scr[...] = jnp.zeros_like(acc_scr)

    def block(j, width=kb, masked=False):
        start = pl.multiple_of(j * kb, kb)
        w_sub = width // CHUNK
        kblk = k_ref[pl.ds(start, width), :]
        vblk = k_ref[pl.ds(start, width), :CHUNK]
        vext = jnp.concatenate([vblk, jnp.ones_like(vblk)], axis=1)
        if masked:
            kpos = start + lax.broadcasted_iota(jnp.int32, (bq, width), 1)
            qpos = qi * bq + lax.broadcasted_iota(jnp.int32, (bq, width), 0)
            visible = kpos <= qpos
        for g0 in range(0, n_heads, FLASH_HEAD_GROUP):
            grp = slice(g0 * bq, (g0 + FLASH_HEAD_GROUP) * bq)
            qg = q_ref[g0:g0 + FLASH_HEAD_GROUP].reshape(FLASH_HEAD_GROUP * bq, dk)
            s_scr[grp, :width] = lax.dot_general(qg, kblk, (((1,), (1,)), ((), ())),
                                                 preferred_element_type=F32)
            for h in range(g0, g0 + FLASH_HEAD_GROUP):
                rows = slice(h * bq, (h + 1) * bq)
                ks = jnp.concatenate([kst_ref[j * n_sub + c, h:h + 1, :] for c in range(w_sub)], axis=1)
                s = s_scr[rows, :width] * ks
                if masked:
                    s = jnp.where(visible, s, NEG_INF)
                m_prev = m_scr[rows, :]
                m_new = jnp.maximum(m_prev, jnp.max(s, axis=-1, keepdims=True))
                alpha = jnp.exp2(m_prev - m_new)
                p = jnp.exp2(s - jnp.concatenate([m_new] * w_sub, axis=1))
                m_scr[rows, :] = m_new
                p_scr[rows, :width] = p.astype(BF16)
                l_scr[rows, :] = l_scr[rows, :] * alpha
                acc_scr[rows, :] = acc_scr[rows, :] * alpha
            pv = jnp.dot(p_scr[grp, :width], vext, preferred_element_type=F32)
            acc_scr[grp, :] += pv[:, :CHUNK]
            l_scr[grp, :] += pv[:, CHUNK:]

    last = (qi * bq) // kb

    def body(i, carry):
        block(2 * i)
        block(2 * i + 1)
        return carry

    lax.fori_loop(0, last // 2, body, 0)

    @pl.when(last % 2 == 1)
    def _():
        block(last - 1)

    diag_chunks = (qi * bq - last * kb) // CHUNK + 1
    for r in range(1, n_sub + 1):
        @pl.when(diag_chunks == r)
        def _():
            block(last, width=r * CHUNK, masked=True)

    for h in range(n_heads):
        rows = slice(h * bq, (h + 1) * bq)
        o_ref[:, h * CHUNK:(h + 1) * CHUNK] = (acc_scr[rows, :] / l_scr[rows, :]).astype(BF16)


def _flash(q, kmat, kst, o_s, batch, seq):
    n_heads, t_total, dk = q.shape
    bq = CHUNK
    nq = seq // bq
    kb = KV_BLOCK
    assert o_s.shape[0] == bq and t_total == batch * seq + bq

    def kv_block(g):
        return jnp.minimum(g // nq, batch - 1)

    return pl.pallas_call(
        functools.partial(_flash_kernel, nq=nq),
        grid=(batch * nq + 1,),
        in_specs=[
            pl.BlockSpec((n_heads, bq, dk), lambda g: (0, g, 0)),
            pl.BlockSpec((seq, dk), lambda g: (kv_block(g), 0)),
            pl.BlockSpec((seq // CHUNK, n_heads, CHUNK), lambda g: (kv_block(g), 0, 0)),
            pl.BlockSpec(o_s.shape, lambda g: (0, 0)),
        ],
        out_specs=pl.BlockSpec((bq, n_heads * CHUNK), lambda g: (g, 0)),
        out_shape=jax.ShapeDtypeStruct((t_total, n_heads * CHUNK), BF16),
        scratch_shapes=[
            pltpu.VMEM((n_heads * bq, kb), F32),
            pltpu.VMEM((n_heads * bq, kb), BF16),
            pltpu.VMEM((n_heads * bq, LANE), F32),
            pltpu.VMEM((n_heads * bq, LANE), F32),
            pltpu.VMEM((n_heads * bq, CHUNK), F32),
        ],
        compiler_params=_params("arbitrary"),
        name="flash_prompt",
    )(q, kmat, kst, o_s)


def _decode_kernel(pt_ref, q_ref, row_ref, cache_ref, o_ref, kbuf, sem, *, layer_slot, rope_dim, sm_scale):
    b = pl.program_id(0)
    nb = pl.num_programs(0)
    n_pages = pt_ref.shape[1]
    cache_dim, page = cache_ref.shape[2:]
    n_group, n_heads, dk = q_ref.shape
    qk_dim = LANE + rope_dim
    slot = b % 2

    def page_copy(pool_page, u, p, s):
        return pltpu.make_async_copy(
            cache_ref.at[layer_slot, pool_page],
            kbuf.at[s, u, pl.ds(0, cache_dim), pl.ds(pl.multiple_of(p * page, page), page)],
            sem.at[s])

    def start_all(step, s):
        for u in range(n_group):
            def go(p, carry):
                page_copy(pt_ref[step * n_group + u, p], u, p, s).start()
                return carry
            lax.fori_loop(0, n_pages, go, 0, unroll=8)

    @pl.when(b == 0)
    def _():
        kbuf[:, :, cache_dim:, :] = jnp.zeros((2, n_group, dk - cache_dim, kbuf.shape[3]), F32)
        start_all(0, 0)

    @pl.when(b + 1 < nb)
    def _():
        start_all(b + 1, 1 - slot)

    for u in range(n_group):
        for p in range(n_pages):
            page_copy(0, u, p, slot).wait()

    for u in range(n_group):
        kt = kbuf[slot, u]
        kt16 = kt.astype(BF16)
        q = q_ref[u]
        s = jnp.dot(q, kt16, preferred_element_type=F32)
        s = (s * kt[qk_dim:qk_dim + n_heads, :]) * sm_scale

        row = row_ref[u]
        qf = q.astype(F32)
        lane = lax.broadcasted_iota(jnp.int32, qf.shape, 1)
        head = lax.broadcasted_iota(jnp.int32, qf.shape, 0)
        ks_new = jnp.sum(jnp.where(lane == qk_dim + head, row, 0.0), axis=-1, keepdims=True)
        s_new = (jnp.sum(qf * row, axis=-1, keepdims=True) * ks_new) * sm_scale

        m = jnp.maximum(jnp.max(s, axis=-1, keepdims=True), s_new)
        p = jnp.exp(s - m)
        p_new = jnp.exp(s_new - m)
        denom = jnp.sum(p, axis=-1, keepdims=True) + p_new
        pv = lax.dot_general(p.astype(BF16), kt16[:LANE, :], (((1,), (1,)), ((), ())),
                             preferred_element_type=F32)
        o_ref[u] = (pv + p_new * row[:, :LANE]) / denom


def _decode(page_table, q_s, rows_s, cache_t, layer_slot, rope_dim, qk_dim):
    n_seq, n_heads, dk = q_s.shape
    past = page_table.shape[1] * cache_t.shape[3]
    grp = DECODE_SEQS
    kern = functools.partial(_decode_kernel, layer_slot=layer_slot, rope_dim=rope_dim,
                             sm_scale=qk_dim ** -0.5)
    grid_spec = pltpu.PrefetchScalarGridSpec(
        num_scalar_prefetch=1,
        grid=(n_seq // grp,),
        in_specs=[
            pl.BlockSpec((grp, n_heads, dk), lambda b, pt: (b, 0, 0)),
            pl.BlockSpec((grp, 1, dk), lambda b, pt: (b, 0, 0)),
            pl.BlockSpec(memory_space=pl.ANY),
        ],
        out_specs=pl.BlockSpec((grp, n_heads, LANE), lambda b, pt: (b, 0, 0)),
        scratch_shapes=[
            pltpu.VMEM((2, grp, dk, past), F32),
            pltpu.SemaphoreType.DMA((2,)),
        ],
    )
    return pl.pallas_call(
        kern,
        grid_spec=grid_spec,
        out_shape=jax.ShapeDtypeStruct((n_seq, n_heads, LANE), F32),
        compiler_params=_params("arbitrary"),
        name="decode_attn",
    )(page_table, q_s, rows_s, cache_t)


def _attn_out_kernel(x_ref, o_ref, wuv_ref, wo_ref, y_ref, ov_scr):
    n_heads = wuv_ref.shape[0]
    for h in range(n_heads):
        cols = slice(h * LANE, (h + 1) * LANE)
        ov_scr[:, cols] = jnp.dot(o_ref[:, cols], wuv_ref[h], preferred_element_type=F32).astype(BF16)
    y_ref[...] = x_ref[...] + jnp.dot(ov_scr[...], wo_ref[...], preferred_element_type=F32)


def _attn_out(x, o_all, wuv, wo, j):
    t, d = x.shape
    tm = TOKEN_TILE
    n_heads = wuv.shape[1]
    od = o_all.shape[1]
    return pl.pallas_call(
        _attn_out_kernel,
        grid=(t // tm,),
        in_specs=[
            pl.BlockSpec((tm, d), lambda i: (i, 0)),
            pl.BlockSpec((tm, od), lambda i: (i, 0)),
            _resident((None, n_heads, LANE, LANE), lambda i: (j, 0, 0, 0)),
            _resident((None, od, d), lambda i: (j, 0, 0)),
        ],
        out_specs=pl.BlockSpec((tm, d), lambda i: (i, 0)),
        out_shape=jax.ShapeDtypeStruct((t, d), F32),
        scratch_shapes=[pltpu.VMEM((tm, od), BF16)],
        compiler_params=_params("arbitrary"),
        name="attn_out",
    )(x, o_all, wuv, wo)


def _rope_tables(pos, rope_dim, theta=10000.0):
    half = rope_dim // 2
    inv = theta ** (-jnp.arange(half, dtype=F32) / half)
    ang = pos.astype(F32)[:, None] * inv[None, :]
    cos, sin = jnp.cos(ang), jnp.sin(ang)
    zeros = jnp.zeros((pos.shape[0], LANE - rope_dim), F32)
    return (jnp.concatenate([cos, cos, zeros], axis=1),
            jnp.concatenate([-sin, sin, zeros], axis=1))


def _rope_cols(w, nope, half):
    lead = w.shape[:-1]
    pad_n = jnp.zeros(lead + (LANE - nope,), w.dtype)
    pad_r = jnp.zeros(lead + (LANE - 2 * half,), w.dtype)
    r1, r2 = w[..., nope:nope + half], w[..., nope + half:]
    return jnp.concatenate([w[..., :nope], pad_n, r1, r2, pad_r, r2, r1, pad_r], axis=-1)


def kernel(x_prompt, x_sample, cache_mla, page_table, norm_g, ffn_w_gate, ffn_w_up, ffn_w_down,
           gmlp_w_in, gmlp_b_in, gmlp_v_norm, gmlp_w_s, gmlp_b_s, gmlp_w_out,
           mla_w_dq, mla_q_norm, mla_w_uq, mla_w_dkv, mla_kv_norm, mla_w_uk, mla_w_uv,
           mla_qk_gain_q, mla_qk_gain_k, mla_w_o):
    batch, seq, d = x_prompt.shape
    n_sample = x_sample.shape[0]
    depth = norm_g.shape[0]
    n_a, n_b = gmlp_w_in.shape[0], mla_w_dq.shape[0]
    d_ff = ffn_w_gate.shape[3]
    d_gmlp = gmlp_w_out.shape[1]
    n_groups = gmlp_w_s.shape[1]
    gd = d_gmlp // n_groups
    n_heads, qk_dim = mla_w_uq.shape[2], mla_w_uq.shape[3]
    kv_lora = mla_kv_norm.shape[1]
    nope = mla_w_uk.shape[3]
    rope_dim = qk_dim - nope
    half = rope_dim // 2
    past = page_table.shape[1] * cache_mla.shape[2]
    n_prompt = batch * seq
    t = n_prompt + n_sample
    assert x_sample.shape[1] == 1 and n_sample == CHUNK and kv_lora == LANE and nope == LANE
    assert t % TOKEN_TILE == 0 and seq % KV_BLOCK == 0 and d_ff % MXU_N == 0

    wg = ffn_w_gate.astype(BF16)
    wu = ffn_w_up.astype(BF16)
    wd = ffn_w_down.astype(BF16)
    gains = norm_g.reshape(depth * 3, 1, d)

    w_in = gmlp_w_in.astype(BF16).reshape(n_a, d, 2, n_groups, gd).transpose(0, 2, 3, 1, 4)
    g_wu, g_wv = w_in[:, 0], w_in[:, 1]
    b_in = gmlp_b_in.reshape(n_a, 2, n_groups, 1, gd)
    g_bu, g_bv = b_in[:, 0], b_in[:, 1]
    g_vg = gmlp_v_norm.reshape(n_a, n_groups, 1, gd)
    tril = jnp.tril(jnp.ones((CHUNK, CHUNK), dtype=bool))
    ws_prompt = jnp.where(tril, gmlp_w_s, 0.0)
    ws_sample = jnp.eye(CHUNK, dtype=F32) * gmlp_w_s[:, :, :1, :1]
    g_ws = jnp.stack([ws_prompt, ws_sample], axis=1).astype(BF16)
    bs_sample = jnp.broadcast_to(gmlp_b_s[:, :, :1], gmlp_b_s.shape)
    g_bs = jnp.stack([gmlp_b_s, bs_sample], axis=1)[..., None]
    g_wo = gmlp_w_out.astype(BF16).reshape(n_a, n_groups, gd, d)

    def gain_pair(a, b):
        pad = jnp.zeros((n_b, LANE - rope_dim), F32)
        gn = jnp.stack([a[:, :nope], b[:, :nope]], axis=1)
        gr = jnp.stack([jnp.concatenate([a[:, nope:], pad], 1), jnp.concatenate([b[:, nope:], pad], 1)], axis=1)
        return gn, gr
    gn, gr = gain_pair(mla_qk_gain_q, mla_qk_gain_k)
    mla = {
        "wdq": mla_w_dq.astype(BF16),
        "qg": mla_q_norm[:, None, :],
        "wuq": _rope_cols(mla_w_uq.transpose(0, 2, 1, 3), nope, half).astype(BF16),
        "wdkv": _rope_cols(mla_w_dkv, kv_lora, half).astype(BF16),
        "kvg": mla_kv_norm[:, None, :],
        "wuk": mla_w_uk.reshape(n_b, kv_lora, n_heads * nope).astype(BF16),
        "wukt": mla_w_uk.transpose(0, 2, 3, 1).astype(BF16),
        "gn": gn,
        "gr": gr,
    }
    m_wuv = mla_w_uv.transpose(0, 2, 1, 3).astype(BF16)
    m_wo = mla_w_o.astype(BF16)

    pos = jnp.concatenate([jnp.tile(jnp.arange(seq, dtype=jnp.int32), batch),
                           jnp.full((n_sample,), past, jnp.int32)])
    cosc, sins = _rope_tables(pos, rope_dim)

    cache_t = jnp.swapaxes(cache_mla, 2, 3)

    x = jnp.concatenate([x_prompt.reshape(n_prompt, d), x_sample.reshape(n_sample, d)], axis=0)
    new_rows, new_v = [], []
    for i in range(depth):
        x = _ffn(x, gains, wg, wu, wd, i, 0)
        j = i // 2
        if i % 2 == 0:
            x, v_s = _gmlp(x, gains, g_wu, g_wv, g_bu, g_bv, g_vg, g_ws, g_bs, g_wo, i, j,
                           n_sample, n_prompt // CHUNK)
            new_v.append(v_s)
        else:
            q, rows, kmat, kst = _mla_proj(x, gains, cosc, sins, mla, i, j, qk_dim, rope_dim)
            q_s = lax.optimization_barrier(q[:, n_prompt:, :]).transpose(1, 0, 2)
            rows_s = jnp.pad(rows[n_prompt:], ((0, 0), (0, 2 * LANE - rows.shape[1])))[:, None, :]
            o_s = _decode(page_table, q_s, rows_s, cache_t, j, rope_dim, qk_dim)
            o_all = _flash(q, kmat, kst, o_s.reshape(n_sample, -1), batch, seq)
            x = _attn_out(x, o_all, m_wuv, m_wo, j)
            new_rows.append(rows)
        x = _ffn(x, gains, wg, wu, wd, i, 1)

    rows = jnp.stack(new_rows)
    return (x[:n_prompt].reshape(batch, seq, d),
            x[n_prompt:].reshape(n_sample, 1, d),
            rows[:, :n_prompt].reshape(n_b, batch, seq, -1),
            rows[:, n_prompt:].reshape(n_b, n_sample, 1, -1),
            jnp.stack(new_v).reshape(n_a, n_sample, 1, d_gmlp))
```

```python
import functools

import jax
import jax.numpy as jnp
from jax import lax
from jax.experimental import pallas as pl
from jax.experimental.pallas import tpu as pltpu

F32 = jnp.float32
BF16 = jnp.bfloat16
EPS = 1e-6
NEG_INF = -1e30

LANE = 128
MXU_N = 256
CHUNK = 128
TOKEN_TILE = 384
KV_BLOCK = 512
FLASH_HEAD_GROUP = 2
DECODE_SEQS = 2
LOG2_E = 1.4426950408889634
VMEM_LIMIT = 56 * 1024 * 1024


def _params(*sem):
    return pltpu.CompilerParams(dimension_semantics=sem, vmem_limit_bytes=VMEM_LIMIT)


def _resident(shape, index_map):
    return pl.BlockSpec(shape, index_map, pipeline_mode=pl.Buffered(1))


def _rms(x, gain):
    return (x * lax.rsqrt(jnp.mean(x * x, axis=-1, keepdims=True) + EPS)) * gain


def _ffn_kernel(*refs, n_in_chunks, split_out, n_prompt_chunks):
    n_x = n_in_chunks + 1 if n_in_chunks else 1
    x_refs, (g_ref, wg_ref, wu_ref, wd_ref) = refs[:n_x], refs[n_x:n_x + 4]
    out_refs, (h_scr, acc_scr) = refs[n_x + 4:-2], refs[-2:]
    i = pl.program_id(0)
    if n_in_chunks:
        xs_ref = x_refs[-1]
        x = jnp.concatenate(
            [jnp.where(i * n_in_chunks + c >= n_prompt_chunks, xs_ref[...], x_refs[c][...])
             for c in range(n_in_chunks)], axis=0)
    else:
        x = x_refs[0][...]
    h_scr[...] = _rms(x, g_ref[...]).astype(BF16)
    acc_scr[...] = jnp.zeros_like(acc_scr)

    for c in range(0, wg_ref.shape[1], MXU_N):
        cols = slice(c, c + MXU_N)
        h = h_scr[...]
        gate = jnp.dot(h, wg_ref[:, cols], preferred_element_type=F32)
        up = jnp.dot(h, wu_ref[:, cols], preferred_element_type=F32)
        act = (jax.nn.silu(gate) * up).astype(BF16)
        acc_scr[...] += jnp.dot(act, wd_ref[cols, :], preferred_element_type=F32)
    y = x + 0.5 * acc_scr[...]
    out_refs[0][...] = y
    if split_out:
        out_refs[1][...] = y[y.shape[0] - out_refs[1].shape[0]:, :]


def _ffn(x, gains, wg, wu, wd, layer, k, split_out=False):
    tm = TOKEN_TILE
    d_ff = wg.shape[3]
    if isinstance(x, tuple):
        xp, xs = x
        n_prompt, d = xp.shape
        n_sample = xs.shape[0]
        assert n_sample == CHUNK and tm % CHUNK == 0
        n_in_chunks = tm // CHUNK
        last_chunk = n_prompt // CHUNK - 1
        x_args = [xp] * n_in_chunks + [xs]
        x_specs = [pl.BlockSpec((CHUNK, d), functools.partial(
            lambda i, c: (jnp.minimum(i * n_in_chunks + c, last_chunk), 0), c=c)) for c in range(n_in_chunks)]
        x_specs.append(pl.BlockSpec((n_sample, d), lambda i: (0, 0)))
        t = n_prompt + n_sample
    else:
        t, d = x.shape
        n_in_chunks = 0
        x_args, x_specs = [x], [pl.BlockSpec((tm, d), lambda i: (i, 0))]
    if split_out:
        n_sample = CHUNK
        n_prompt = t - n_sample
        out_specs = [pl.BlockSpec((tm, d), lambda i: (i, 0)), pl.BlockSpec((n_sample, d), lambda i: (0, 0))]
        out_shape = [jax.ShapeDtypeStruct((n_prompt, d), F32), jax.ShapeDtypeStruct((n_sample, d), F32)]
    else:
        out_specs = pl.BlockSpec((tm, d), lambda i: (i, 0))
        out_shape = jax.ShapeDtypeStruct((t, d), F32)
    kern = functools.partial(_ffn_kernel, n_in_chunks=n_in_chunks, split_out=split_out,
                             n_prompt_chunks=(t - CHUNK) // CHUNK)
    return pl.pallas_call(
        kern,
        grid=(t // tm,),
        in_specs=x_specs + [
            pl.BlockSpec((None, 1, d), lambda i: (3 * layer + 2 * k, 0, 0)),
            _resident((None, None, d, d_ff), lambda i: (layer, k, 0, 0)),
            _resident((None, None, d, d_ff), lambda i: (layer, k, 0, 0)),
            _resident((None, None, d_ff, d), lambda i: (layer, k, 0, 0)),
        ],
        out_specs=out_specs,
        out_shape=out_shape,
        scratch_shapes=[pltpu.VMEM((tm, d), BF16), pltpu.VMEM((tm, d), F32)],
        compiler_params=_params("arbitrary"),
        name="ffn_half",
    )(*x_args, gains, wg, wu, wd)


def _gmlp_kernel(x_ref, g_ref, wu_ref, wv_ref, bu_ref, bv_ref, vg_ref, ws_ref, bs_ref, wo_ref,
                 o_ref, vs_ref, h_scr, v_scr, vb_scr, *, n_prompt_chunks):
    i = pl.program_id(0)
    tm = x_ref.shape[0]
    n_groups, _, gd = wu_ref.shape
    n_sub = tm // CHUNK
    x = x_ref[...]
    h_scr[...] = _rms(x, g_ref[...]).astype(BF16)

    ssq = jnp.zeros((tm, 1), F32)
    for g in range(n_groups):
        z = jnp.dot(h_scr[...], wv_ref[g], preferred_element_type=F32) + bv_ref[g]
        v = jax.nn.gelu(z, approximate=True)
        v_scr[:, g * gd:(g + 1) * gd] = v
        ssq = ssq + jnp.sum(v * v, axis=-1, keepdims=True)
    inv = lax.rsqrt(ssq / (n_groups * gd) + EPS)
    for g in range(n_groups):
        vn = (v_scr[:, g * gd:(g + 1) * gd] * inv) * vg_ref[g]
        v_scr[:, g * gd:(g + 1) * gd] = vn
        vb_scr[:, g * gd:(g + 1) * gd] = vn.astype(BF16)

    for g in range(n_groups):
        z = jnp.dot(h_scr[...], wu_ref[g], preferred_element_type=F32) + bu_ref[g]
        u = jax.nn.gelu(z, approximate=True)
        mixed = []
        for c in range(n_sub):
            sel = ((i * n_sub + c) >= n_prompt_chunks).astype(jnp.int32)
            vc = vb_scr[c * CHUNK:(c + 1) * CHUNK, g * gd:(g + 1) * gd]
            s = jnp.dot(ws_ref[sel, g], vc, preferred_element_type=F32) + bs_ref[sel, g]
            mixed.append(s)
        vb_scr[:, g * gd:(g + 1) * gd] = (u * jnp.concatenate(mixed, axis=0)).astype(BF16)
    o_ref[...] = x + jnp.dot(vb_scr[...], wo_ref[...], preferred_element_type=F32)

    @pl.when(i == pl.num_programs(0) - 1)
    def _():
        vs_ref[...] = v_scr[tm - vs_ref.shape[0]:, :]


def _gmlp(x, gains, wu, wv, bu, bv, vg, ws, bs, wo, layer, j, n_sample, n_prompt_chunks):
    t, d = x.shape
    ng, _, gd = wu.shape[1:]
    tm = TOKEN_TILE
    kern = functools.partial(_gmlp_kernel, n_prompt_chunks=n_prompt_chunks)
    return pl.pallas_call(
        kern,
        grid=(t // tm,),
        in_specs=[
            pl.BlockSpec((tm, d), lambda i: (i, 0)),
            pl.BlockSpec((None, 1, d), lambda i: (3 * layer + 1, 0, 0)),
            _resident((None, ng, d, gd), lambda i: (j, 0, 0, 0)),
            _resident((None, ng, d, gd), lambda i: (j, 0, 0, 0)),
            _resident((None, ng, 1, gd), lambda i: (j, 0, 0, 0)),
            _resident((None, ng, 1, gd), lambda i: (j, 0, 0, 0)),
            _resident((None, ng, 1, gd), lambda i: (j, 0, 0, 0)),
            _resident((None, 2, ng, CHUNK, CHUNK), lambda i: (j, 0, 0, 0, 0)),
            _resident((None, 2, ng, CHUNK, 1), lambda i: (j, 0, 0, 0, 0)),
            _resident((None, ng * gd, d), lambda i: (j, 0, 0)),
        ],
        out_specs=[
            pl.BlockSpec((tm, d), lambda i: (i, 0)),
            pl.BlockSpec((n_sample, ng * gd), lambda i: (0, 0)),
        ],
        out_shape=[
            jax.ShapeDtypeStruct((t, d), F32),
            jax.ShapeDtypeStruct((n_sample, ng * gd), F32),
        ],
        scratch_shapes=[
            pltpu.VMEM((tm, d), BF16),
            pltpu.VMEM((tm, ng * gd), F32),
            pltpu.VMEM((tm, ng * gd), BF16),
        ],
        compiler_params=_params("arbitrary"),
        name="gmlp_mixer",
    )(x, gains, wu, wv, bu, bv, vg, ws, bs, wo)


def _mla_proj_kernel(x_ref, g_ref, cosc_ref, sins_ref, wdq_ref, qg_ref, wuq_ref, wdkv_ref, kvg_ref,
                     wuk_ref, wukt_ref, gr_ref,
                     q_ref, rows_ref, k_ref, kst_ref, qs_ref, *, qk_dim, rope_dim, kst_scale):
    n_heads = wuq_ref.shape[0]
    hb = _rms(x_ref[...], g_ref[...]).astype(BF16)
    cosc = cosc_ref[...]
    sins = sins_ref[...]
    tm = hb.shape[0]

    ckv = jnp.dot(hb, wdkv_ref[...], preferred_element_type=F32)
    c = _rms(ckv[:, :LANE], kvg_ref[...])
    kr = ckv[:, LANE:2 * LANE] * cosc + ckv[:, 2 * LANE:] * sins
    k_nope = jnp.dot(c.astype(BF16), wuk_ref[...], preferred_element_type=F32)
    ssq_r = jnp.sum(kr * kr, axis=-1, keepdims=True)
    lane = lax.broadcasted_iota(jnp.int32, (tm, LANE), 1)
    tail = kr
    for h in range(n_heads):
        kn = k_nope[:, h * LANE:(h + 1) * LANE]
        ssq = jnp.sum(kn * kn, axis=-1, keepdims=True) + ssq_r
        ks = lax.rsqrt(ssq / qk_dim + EPS)
        tail = jnp.where(lane == rope_dim + h, ks, tail)
    rows_ref[:, :LANE] = c
    rows_ref[:, LANE:] = tail[:, :rows_ref.shape[1] - LANE]
    k_ref[:, :LANE] = c.astype(BF16)
    k_ref[:, LANE:] = tail.astype(BF16)
    tail_t = jnp.transpose(tail)
    for cidx in range(tm // CHUNK):
        kst_ref[cidx] = tail_t[rope_dim:rope_dim + n_heads, cidx * CHUNK:(cidx + 1) * CHUNK] * kst_scale

    cq = _rms(jnp.dot(hb, wdq_ref[...], preferred_element_type=F32), qg_ref[...]).astype(BF16)
    gr = gr_ref[...]
    n_tail = qs_ref.shape[1]
    for h in range(n_heads):
        qh = jnp.dot(cq, wuq_ref[h], preferred_element_type=F32)
        qn = qh[:, :LANE]
        qr = qh[:, LANE:2 * LANE] * cosc + qh[:, 2 * LANE:] * sins
        ssq = jnp.sum(qn * qn + qr * qr, axis=-1, keepdims=True)
        inv = lax.rsqrt(ssq / qk_dim + EPS)
        q_lat = (jnp.dot(qn.astype(BF16), wukt_ref[h], preferred_element_type=F32) * inv).astype(BF16)
        q_rope = ((qr * inv) * gr).astype(BF16)
        q_ref[h, :, :LANE] = q_lat
        q_ref[h, :, LANE:] = q_rope
        qs_ref[h, :, :LANE] = q_lat[tm - n_tail:, :]
        qs_ref[h, :, LANE:] = q_rope[tm - n_tail:, :]


def _mla_proj(x, gains, cosc, sins, w, layer, j, qk_dim, rope_dim, n_sample):
    t, d = x.shape
    tm = TOKEN_TILE
    n_heads = w["wuq"].shape[1]
    q_lora = w["wdq"].shape[2]
    cache_dim = LANE + rope_dim + n_heads
    kern = functools.partial(_mla_proj_kernel, qk_dim=qk_dim, rope_dim=rope_dim,
                             kst_scale=qk_dim ** -0.5 * LOG2_E)
    return pl.pallas_call(
        kern,
        grid=(t // tm,),
        in_specs=[
            pl.BlockSpec((tm, d), lambda i: (i, 0)),
            pl.BlockSpec((None, 1, d), lambda i: (3 * layer + 1, 0, 0)),
            pl.BlockSpec((tm, LANE), lambda i: (i, 0)),
            pl.BlockSpec((tm, LANE), lambda i: (i, 0)),
            _resident((None, d, q_lora), lambda i: (j, 0, 0)),
            _resident((None, 1, q_lora), lambda i: (j, 0, 0)),
            _resident((None, n_heads, q_lora, 3 * LANE), lambda i: (j, 0, 0, 0)),
            _resident((None, d, 3 * LANE), lambda i: (j, 0, 0)),
            _resident((None, 1, LANE), lambda i: (j, 0, 0)),
            _resident((None, LANE, n_heads * LANE), lambda i: (j, 0, 0)),
            _resident((None, n_heads, LANE, LANE), lambda i: (j, 0, 0, 0)),
            _resident((None, 1, LANE), lambda i: (j, 0, 0)),
        ],
        out_specs=[
            pl.BlockSpec((n_heads, tm, 2 * LANE), lambda i: (0, i, 0)),
            pl.BlockSpec((tm, cache_dim), lambda i: (i, 0)),
            pl.BlockSpec((tm, 2 * LANE), lambda i: (i, 0)),
            pl.BlockSpec((tm // CHUNK, n_heads, CHUNK), lambda i: (i, 0, 0)),
            pl.BlockSpec((n_heads, n_sample, 2 * LANE), lambda i: (0, 0, 0)),
        ],
        out_shape=[
            jax.ShapeDtypeStruct((n_heads, t, 2 * LANE), BF16),
            jax.ShapeDtypeStruct((t, cache_dim), F32),
            jax.ShapeDtypeStruct((t, 2 * LANE), BF16),
            jax.ShapeDtypeStruct((t // CHUNK, n_heads, CHUNK), F32),
            jax.ShapeDtypeStruct((n_heads, n_sample, 2 * LANE), BF16),
        ],
        compiler_params=_params("arbitrary"),
        name="mla_proj",
    )(x, gains, cosc, sins, w["wdq"], w["qg"], w["wuq"], w["wdkv"], w["kvg"], w["wuk"], w["wukt"],
      w["gr"])


def _flash_kernel(q_ref, k_ref, kst_ref, os_ref, o_ref, s_scr, p_scr, m_scr, l_scr, acc_scr, *, nq):
    g = pl.program_id(0)
    n_prompt_blocks = pl.num_programs(0) - 1

    @pl.when(g < n_prompt_blocks)
    def _():
        _flash_block(g % nq, q_ref, k_ref, kst_ref, o_ref, s_scr, p_scr, m_scr, l_scr, acc_scr)

    @pl.when(g == n_prompt_blocks)
    def _():
        o_ref[...] = os_ref[...].astype(BF16)


def _flash_block(qi, q_ref, k_ref, kst_ref, o_ref, s_scr, p_scr, m_scr, l_scr, acc_scr):
    n_heads, bq, dk = q_ref.shape
    kb = s_scr.shape[1]
    n_sub = kb // CHUNK
    m_scr[...] = jnp.full_like(m_scr, NEG_INF)
    l_scr[...] = jnp.zeros_like(l_scr)
    acc_scr[...] = jnp.zeros_like(acc_scr)

    def block(j, width=kb, masked=False):
        start = pl.multiple_of(j * kb, kb)
        w_sub = width // CHUNK
        kblk = k_ref[pl.ds(start, width), :]
        vblk = k_ref[pl.ds(start, width), :CHUNK]
        vext = jnp.concatenate([vblk, jnp.ones_like(vblk)], axis=1)
        if masked:
            kpos = start + lax.broadcasted_iota(jnp.int32, (bq, width), 1)
            qpos = qi * bq + lax.broadcasted_iota(jnp.int32, (bq, width), 0)
            visible = kpos <= qpos
        for g0 in range(0, n_heads, FLASH_HEAD_GROUP):
            grp = slice(g0 * bq, (g0 + FLASH_HEAD_GROUP) * bq)
            qg = q_ref[g0:g0 + FLASH_HEAD_GROUP].reshape(FLASH_HEAD_GROUP * bq, dk)
            s_scr[grp, :width] = lax.dot_general(qg, kblk, (((1,), (1,)), ((), ())),
                                                 preferred_element_type=F32)
            for h in range(g0, g0 + FLASH_HEAD_GROUP):
                rows = slice(h * bq, (h + 1) * bq)
                ks = jnp.concatenate([kst_ref[j * n_sub + c, h:h + 1, :] for c in range(w_sub)], axis=1)
                s = s_scr[rows, :width] * ks
                if masked:
                    s = jnp.where(visible, s, NEG_INF)
                m_prev = m_scr[rows, :]
                m_new = jnp.maximum(m_prev, jnp.max(s, axis=-1, keepdims=True))
                alpha = jnp.exp2(m_prev - m_new)
                p = jnp.exp2(s - jnp.concatenate([m_new] * w_sub, axis=1))
                m_scr[rows, :] = m_new
                p_scr[rows, :width] = p.astype(BF16)
                l_scr[rows, :] = l_scr[rows, :] * alpha
                acc_scr[rows, :] = acc_scr[rows, :] * alpha
            pv = jnp.dot(p_scr[grp, :width], vext, preferred_element_type=F32)
            acc_scr[grp, :] += pv[:, :CHUNK]
            l_scr[grp, :] += pv[:, CHUNK:]

    last = (qi * bq) // kb

    def body(i, carry):
        block(2 * i)
        block(2 * i + 1)
        return carry

    lax.fori_loop(0, last // 2, body, 0)

    @pl.when(last % 2 == 1)
    def _():
        block(last - 1)

    diag_chunks = (qi * bq - last * kb) // CHUNK + 1
    for r in range(1, n_sub + 1):
        @pl.when(diag_chunks == r)
        def _():
            block(last, width=r * CHUNK, masked=True)

    for h in range(n_heads):
        rows = slice(h * bq, (h + 1) * bq)
        o_ref[:, h * CHUNK:(h + 1) * CHUNK] = (acc_scr[rows, :] / l_scr[rows, :]).astype(BF16)


def _flash(q, kmat, kst, o_s, batch, seq):
    n_heads, t_total, dk = q.shape
    bq = CHUNK
    nq = seq // bq
    kb = KV_BLOCK
    assert o_s.shape[0] == bq and t_total == batch * seq + bq

    def kv_block(g):
        return jnp.minimum(g // nq, batch - 1)

    return pl.pallas_call(
        functools.partial(_flash_kernel, nq=nq),
        grid=(batch * nq + 1,),
        in_specs=[
            pl.BlockSpec((n_heads, bq, dk), lambda g: (0, g, 0)),
            pl.BlockSpec((seq, dk), lambda g: (kv_block(g), 0)),
            pl.BlockSpec((seq // CHUNK, n_heads, CHUNK), lambda g: (kv_block(g), 0, 0)),
            pl.BlockSpec(o_s.shape, lambda g: (0, 0)),
        ],
        out_specs=pl.BlockSpec((bq, n_heads * CHUNK), lambda g: (g, 0)),
        out_shape=jax.ShapeDtypeStruct((t_total, n_heads * CHUNK), BF16),
        scratch_shapes=[
            pltpu.VMEM((n_heads * bq, kb), F32),
            pltpu.VMEM((n_heads * bq, kb), BF16),
            pltpu.VMEM((n_heads * bq, LANE), F32),
            pltpu.VMEM((n_heads * bq, LANE), F32),
            pltpu.VMEM((n_heads * bq, CHUNK), F32),
        ],
        compiler_params=_params("arbitrary"),
        name="flash_prompt",
    )(q, kmat, kst, o_s)


def _decode_kernel(pt_ref, q_ref, row_ref, cache_ref, o_ref, kbuf, sem, *, layer_slot, rope_dim, sm_scale):
    b = pl.program_id(0)
    nb = pl.num_programs(0)
    n_pages = pt_ref.shape[1]
    cache_dim, page = cache_ref.shape[2:]
    n_group, n_heads, dk = q_ref.shape
    qk_dim = LANE + rope_dim
    slot = b % 2

    def page_copy(pool_page, u, p, s):
        return pltpu.make_async_copy(
            cache_ref.at[layer_slot, pool_page],
            kbuf.at[s, u, pl.ds(0, cache_dim), pl.ds(pl.multiple_of(p * page, page), page)],
            sem.at[s])

    def start_all(step, s):
        for u in range(n_group):
            def go(p, carry):
                page_copy(pt_ref[step * n_group + u, p], u, p, s).start()
                return carry
            lax.fori_loop(0, n_pages, go, 0, unroll=8)

    @pl.when(b == 0)
    def _():
        kbuf[:, :, cache_dim:, :] = jnp.zeros((2, n_group, dk - cache_dim, kbuf.shape[3]), F32)
        start_all(0, 0)

    @pl.when(b + 1 < nb)
    def _():
        start_all(b + 1, 1 - slot)

    for u in range(n_group):
        for p in range(n_pages):
            page_copy(0, u, p, slot).wait()

    for u in range(n_group):
        kt = kbuf[slot, u]
        kt16 = kt.astype(BF16)
        q = q_ref[u]
        s = jnp.dot(q, kt16, preferred_element_type=F32)
        s = (s * kt[qk_dim:qk_dim + n_heads, :]) * sm_scale

        row = row_ref[u]
        qf = q.astype(F32)
        lane = lax.broadcasted_iota(jnp.int32, qf.shape, 1)
        head = lax.broadcasted_iota(jnp.int32, qf.shape, 0)
        ks_new = jnp.sum(jnp.where(lane == qk_dim + head, row, 0.0), axis=-1, keepdims=True)
        s_new = (jnp.sum(qf * row, axis=-1, keepdims=True) * ks_new) * sm_scale

        m = jnp.maximum(jnp.max(s, axis=-1, keepdims=True), s_new)
        p = jnp.exp(s - m)
        p_new = jnp.exp(s_new - m)
        denom = jnp.sum(p, axis=-1, keepdims=True) + p_new
        pv = lax.dot_general(p.astype(BF16), kt16[:LANE, :], (((1,), (1,)), ((), ())),
                             preferred_element_type=F32)
        o_ref[u] = (pv + p_new * row[:, :LANE]) / denom


def _decode(page_table, q_s, rows_s, cache_t, layer_slot, rope_dim, qk_dim):
    n_seq, n_heads, dk = q_s.shape
    past = page_table.shape[1] * cache_t.shape[3]
    grp = DECODE_SEQS
    kern = functools.partial(_decode_kernel, layer_slot=layer_slot, rope_dim=rope_dim,
                             sm_scale=qk_dim ** -0.5)
    grid_spec = pltpu.PrefetchScalarGridSpec(
        num_scalar_prefetch=1,
        grid=(n_seq // grp,),
        in_specs=[
            pl.BlockSpec((grp, n_heads, dk), lambda b, pt: (b, 0, 0)),
            pl.BlockSpec((grp, 1, dk), lambda b, pt: (b, 0, 0)),
            pl.BlockSpec(memory_space=pl.ANY),
        ],
        out_specs=pl.BlockSpec((grp, n_heads, LANE), lambda b, pt: (b, 0, 0)),
        scratch_shapes=[
            pltpu.VMEM((2, grp, dk, past), F32),
            pltpu.SemaphoreType.DMA((2,)),
        ],
    )
    return pl.pallas_call(
        kern,
        grid_spec=grid_spec,
        out_shape=jax.ShapeDtypeStruct((n_seq, n_heads, LANE), F32),
        compiler_params=_params("arbitrary"),
        name="decode_attn",
    )(page_table, q_s, rows_s, cache_t)


def _attn_out_kernel(x_ref, o_ref, wuv_ref, wo_ref, y_ref, ov_scr):
    n_heads = wuv_ref.shape[0]
    for h in range(n_heads):
        cols = slice(h * LANE, (h + 1) * LANE)
        ov_scr[:, cols] = jnp.dot(o_ref[:, cols], wuv_ref[h], preferred_element_type=F32).astype(BF16)
    y_ref[...] = x_ref[...] + jnp.dot(ov_scr[...], wo_ref[...], preferred_element_type=F32)


def _attn_out(x, o_all, wuv, wo, j):
    t, d = x.shape
    tm = TOKEN_TILE
    n_heads = wuv.shape[1]
    od = o_all.shape[1]
    return pl.pallas_call(
        _attn_out_kernel,
        grid=(t // tm,),
        in_specs=[
            pl.BlockSpec((tm, d), lambda i: (i, 0)),
            pl.BlockSpec((tm, od), lambda i: (i, 0)),
            _resident((None, n_heads, LANE, LANE), lambda i: (j, 0, 0, 0)),
            _resident((None, od, d), lambda i: (j, 0, 0)),
        ],
        out_specs=pl.BlockSpec((tm, d), lambda i: (i, 0)),
        out_shape=jax.ShapeDtypeStruct((t, d), F32),
        scratch_shapes=[pltpu.VMEM((tm, od), BF16)],
        compiler_params=_params("arbitrary"),
        name="attn_out",
    )(x, o_all, wuv, wo)


def _rope_tables(pos, rope_dim, theta=10000.0):
    half = rope_dim // 2
    inv = theta ** (-jnp.arange(half, dtype=F32) / half)
    ang = pos.astype(F32)[:, None] * inv[None, :]
    cos, sin = jnp.cos(ang), jnp.sin(ang)
    zeros = jnp.zeros((pos.shape[0], LANE - rope_dim), F32)
    return (jnp.concatenate([cos, cos, zeros], axis=1),
            jnp.concatenate([-sin, sin, zeros], axis=1))


def _rope_cols(w, nope, half):
    lead = w.shape[:-1]
    pad_n = jnp.zeros(lead + (LANE - nope,), w.dtype)
    pad_r = jnp.zeros(lead + (LANE - 2 * half,), w.dtype)
    r1, r2 = w[..., nope:nope + half], w[..., nope + half:]
    return jnp.concatenate([w[..., :nope], pad_n, r1, r2, pad_r, r2, r1, pad_r], axis=-1)


def kernel(x_prompt, x_sample, cache_mla, page_table, norm_g, ffn_w_gate, ffn_w_up, ffn_w_down,
           gmlp_w_in, gmlp_b_in, gmlp_v_norm, gmlp_w_s, gmlp_b_s, gmlp_w_out,
           mla_w_dq, mla_q_norm, mla_w_uq, mla_w_dkv, mla_kv_norm, mla_w_uk, mla_w_uv,
           mla_qk_gain_q, mla_qk_gain_k, mla_w_o):
    batch, seq, d = x_prompt.shape
    n_sample = x_sample.shape[0]
    depth = norm_g.shape[0]
    n_a, n_b = gmlp_w_in.shape[0], mla_w_dq.shape[0]
    d_ff = ffn_w_gate.shape[3]
    d_gmlp = gmlp_w_out.shape[1]
    n_groups = gmlp_w_s.shape[1]
    gd = d_gmlp // n_groups
    n_heads, qk_dim = mla_w_uq.shape[2], mla_w_uq.shape[3]
    kv_lora = mla_kv_norm.shape[1]
    nope = mla_w_uk.shape[3]
    rope_dim = qk_dim - nope
    half = rope_dim // 2
    past = page_table.shape[1] * cache_mla.shape[2]
    n_prompt = batch * seq
    t = n_prompt + n_sample
    assert x_sample.shape[1] == 1 and n_sample == CHUNK and kv_lora == LANE and nope == LANE
    assert t % TOKEN_TILE == 0 and seq % KV_BLOCK == 0 and d_ff % MXU_N == 0

    wg = ffn_w_gate.astype(BF16)
    wu = ffn_w_up.astype(BF16)
    wd = ffn_w_down.astype(BF16)
    gains = norm_g.reshape(depth * 3, 1, d)

    w_in = gmlp_w_in.astype(BF16).reshape(n_a, d, 2, n_groups, gd).transpose(0, 2, 3, 1, 4)
    g_wu, g_wv = w_in[:, 0], w_in[:, 1]
    b_in = gmlp_b_in.reshape(n_a, 2, n_groups, 1, gd)
    g_bu, g_bv = b_in[:, 0], b_in[:, 1]
    g_vg = gmlp_v_norm.reshape(n_a, n_groups, 1, gd)
    tril = jnp.tril(jnp.ones((CHUNK, CHUNK), dtype=bool))
    ws_prompt = jnp.where(tril, gmlp_w_s, 0.0)
    ws_sample = jnp.eye(CHUNK, dtype=F32) * gmlp_w_s[:, :, :1, :1]
    g_ws = jnp.stack([ws_prompt, ws_sample], axis=1).astype(BF16)
    bs_sample = jnp.broadcast_to(gmlp_b_s[:, :, :1], gmlp_b_s.shape)
    g_bs = jnp.stack([gmlp_b_s, bs_sample], axis=1)[..., None]
    g_wo = gmlp_w_out.astype(BF16)

    qk_gain = mla_qk_gain_q * mla_qk_gain_k
    gr = jnp.concatenate([qk_gain[:, nope:], jnp.zeros((n_b, LANE - rope_dim), F32)], axis=1)[:, None, :]
    mla = {
        "wdq": mla_w_dq.astype(BF16),
        "qg": mla_q_norm[:, None, :],
        "wuq": _rope_cols(mla_w_uq.transpose(0, 2, 1, 3), nope, half).astype(BF16),
        "wdkv": _rope_cols(mla_w_dkv, kv_lora, half).astype(BF16),
        "kvg": mla_kv_norm[:, None, :],
        "wuk": mla_w_uk.reshape(n_b, kv_lora, n_heads * nope).astype(BF16),
        "wukt": (mla_w_uk.transpose(0, 2, 3, 1) * qk_gain[:, None, :nope, None]).astype(BF16),
        "gr": gr,
    }
    m_wuv = mla_w_uv.transpose(0, 2, 1, 3).astype(BF16)
    m_wo = mla_w_o.astype(BF16)

    pos = jnp.concatenate([jnp.tile(jnp.arange(seq, dtype=jnp.int32), batch),
                           jnp.full((n_sample,), past, jnp.int32)])
    cosc, sins = _rope_tables(pos, rope_dim)

    cache_t = jnp.swapaxes(cache_mla, 2, 3)

    x = (x_prompt.reshape(n_prompt, d), x_sample.reshape(n_sample, d))
    new_rows, new_v = [], []
    for i in range(depth):
        x = _ffn(x, gains, wg, wu, wd, i, 0)
        j = i // 2
        if i % 2 == 0:
            x, v_s = _gmlp(x, gains, g_wu, g_wv, g_bu, g_bv, g_vg, g_ws, g_bs, g_wo, i, j,
                           n_sample, n_prompt // CHUNK)
            new_v.append(v_s)
        else:
            q, rows, kmat, kst, q_tail = _mla_proj(x, gains, cosc, sins, mla, i, j, qk_dim, rope_dim, n_sample)
            q_s = q_tail.transpose(1, 0, 2)
            rows_s = jnp.pad(rows[n_prompt:], ((0, 0), (0, 2 * LANE - rows.shape[1])))[:, None, :]
            o_s = _decode(page_table, q_s, rows_s, cache_t, j, rope_dim, qk_dim)
            o_all = _flash(q, kmat, kst, o_s.reshape(n_sample, -1), batch, seq)
            x = _attn_out(x, o_all, m_wuv, m_wo, j)
            new_rows.append(rows)
        x = _ffn(x, gains, wg, wu, wd, i, 1, split_out=(i == depth - 1))

    y_prompt, y_sample = x
    rows = jnp.stack(new_rows)
    return (y_prompt.reshape(batch, seq, d),
            y_sample.reshape(n_sample, 1, d),
            rows[:, :n_prompt].reshape(n_b, batch, seq, -1),
            rows[:, n_prompt:].reshape(n_b, n_sample, 1, -1),
            jnp.stack(new_v).reshape(n_a, n_sample, 1, d_gmlp))
```

```python
import functools

import jax
import jax.numpy as jnp
from jax import lax
from jax.experimental import pallas as pl
from jax.experimental.pallas import tpu as pltpu

F32 = jnp.float32
BF16 = jnp.bfloat16
EPS = 1e-6
NEG_INF = -1e30

LANE = 128
MXU_N = 256
CHUNK = 128
TOKEN_TILE = 384
KV_BLOCK = 512
FLASH_DIAG_VARIANTS = 4
FLASH_HEAD_GROUP = 2
DECODE_SEQS = 2
LOG2_E = 1.4426950408889634
VMEM_LIMIT = 56 * 1024 * 1024


def _params(*sem):
    return pltpu.CompilerParams(dimension_semantics=sem, vmem_limit_bytes=VMEM_LIMIT)


def _resident(shape, index_map):
    return pl.BlockSpec(shape, index_map, pipeline_mode=pl.Buffered(1))


def _rms(x, gain):
    return (x * lax.rsqrt(jnp.mean(x * x, axis=-1, keepdims=True) + EPS)) * gain


def _ffn_kernel(*refs, n_in_chunks, split_out, n_prompt_chunks):
    n_x = n_in_chunks + 1 if n_in_chunks else 1
    x_refs, (g_ref, wg_ref, wu_ref, wd_ref) = refs[:n_x], refs[n_x:n_x + 4]
    out_refs, (h_scr, acc_scr) = refs[n_x + 4:-2], refs[-2:]
    i = pl.program_id(0)
    if n_in_chunks:
        xs_ref = x_refs[-1]
        x = jnp.concatenate(
            [jnp.where(i * n_in_chunks + c >= n_prompt_chunks, xs_ref[...], x_refs[c][...])
             for c in range(n_in_chunks)], axis=0)
    else:
        x = x_refs[0][...]
    h_scr[...] = _rms(x, g_ref[...]).astype(BF16)
    acc_scr[...] = jnp.zeros_like(acc_scr)

    for c in range(0, wg_ref.shape[1], MXU_N):
        cols = slice(c, c + MXU_N)
        h = h_scr[...]
        gate = jnp.dot(h, wg_ref[:, cols].astype(BF16), preferred_element_type=F32)
        up = jnp.dot(h, wu_ref[:, cols].astype(BF16), preferred_element_type=F32)
        act = (jax.nn.silu(gate) * up).astype(BF16)
        acc_scr[...] += jnp.dot(act, wd_ref[cols, :].astype(BF16), preferred_element_type=F32)
    y = x + 0.5 * acc_scr[...]
    out_refs[0][...] = y
    if split_out:
        out_refs[1][...] = y[y.shape[0] - out_refs[1].shape[0]:, :]


def _ffn(x, gains, wg, wu, wd, layer, k, split_out=False):
    tm = TOKEN_TILE
    d_ff = wg.shape[3]
    if isinstance(x, tuple):
        xp, xs = x
        n_prompt, d = xp.shape
        n_sample = xs.shape[0]
        assert n_sample == CHUNK and tm % CHUNK == 0
        n_in_chunks = tm // CHUNK
        last_chunk = n_prompt // CHUNK - 1
        x_args = [xp] * n_in_chunks + [xs]
        x_specs = [pl.BlockSpec((CHUNK, d), functools.partial(
            lambda i, c: (jnp.minimum(i * n_in_chunks + c, last_chunk), 0), c=c)) for c in range(n_in_chunks)]
        x_specs.append(pl.BlockSpec((n_sample, d), lambda i: (0, 0)))
        t = n_prompt + n_sample
    else:
        t, d = x.shape
        n_in_chunks = 0
        x_args, x_specs = [x], [pl.BlockSpec((tm, d), lambda i: (i, 0))]
    if split_out:
        n_sample = CHUNK
        n_prompt = t - n_sample
        out_specs = [pl.BlockSpec((tm, d), lambda i: (i, 0)), pl.BlockSpec((n_sample, d), lambda i: (0, 0))]
        out_shape = [jax.ShapeDtypeStruct((n_prompt, d), F32), jax.ShapeDtypeStruct((n_sample, d), F32)]
    else:
        out_specs = pl.BlockSpec((tm, d), lambda i: (i, 0))
        out_shape = jax.ShapeDtypeStruct((t, d), F32)
    kern = functools.partial(_ffn_kernel, n_in_chunks=n_in_chunks, split_out=split_out,
                             n_prompt_chunks=(t - CHUNK) // CHUNK)
    return pl.pallas_call(
        kern,
        grid=(t // tm,),
        in_specs=x_specs + [
            pl.BlockSpec((None, 1, d), lambda i: (3 * layer + 2 * k, 0, 0)),
            _resident((None, None, d, d_ff), lambda i: (layer, k, 0, 0)),
            _resident((None, None, d, d_ff), lambda i: (layer, k, 0, 0)),
            _resident((None, None, d_ff, d), lambda i: (layer, k, 0, 0)),
        ],
        out_specs=out_specs,
        out_shape=out_shape,
        scratch_shapes=[pltpu.VMEM((tm, d), BF16), pltpu.VMEM((tm, d), F32)],
        compiler_params=_params("arbitrary"),
        name="ffn_half",
    )(*x_args, gains, wg, wu, wd)


def _gmlp_kernel(x_ref, g_ref, win_ref, bin_ref, vg_ref, ws_ref, bs_ref, wo_ref,
                 o_ref, vs_ref, h_scr, v_scr, vb_scr, *, n_prompt_chunks):
    i = pl.program_id(0)
    tm = x_ref.shape[0]
    n_groups = ws_ref.shape[1]
    d_gmlp = wo_ref.shape[0]
    gd = d_gmlp // n_groups
    n_sub = tm // CHUNK
    x = x_ref[...]
    h_scr[...] = _rms(x, g_ref[...]).astype(BF16)

    def pre_act(col0):
        cols = slice(col0, col0 + gd)
        z = jnp.dot(h_scr[...], win_ref[:, cols].astype(BF16), preferred_element_type=F32) + bin_ref[:, cols]
        return jax.nn.gelu(z, approximate=True)

    ssq = jnp.zeros((tm, 1), F32)
    for g in range(n_groups):
        v = pre_act(d_gmlp + g * gd)
        v_scr[:, g * gd:(g + 1) * gd] = v
        ssq = ssq + jnp.sum(v * v, axis=-1, keepdims=True)
    inv = lax.rsqrt(ssq / d_gmlp + EPS)
    for g in range(n_groups):
        vn = (v_scr[:, g * gd:(g + 1) * gd] * inv) * vg_ref[:, g * gd:(g + 1) * gd]
        v_scr[:, g * gd:(g + 1) * gd] = vn
        vb_scr[:, g * gd:(g + 1) * gd] = vn.astype(BF16)

    for g in range(n_groups):
        u = pre_act(g * gd)
        mixed = []
        for c in range(n_sub):
            sel = ((i * n_sub + c) >= n_prompt_chunks).astype(jnp.int32)
            vc = vb_scr[c * CHUNK:(c + 1) * CHUNK, g * gd:(g + 1) * gd]
            s = jnp.dot(ws_ref[sel, g], vc, preferred_element_type=F32) + bs_ref[sel, g]
            mixed.append(s)
        vb_scr[:, g * gd:(g + 1) * gd] = (u * jnp.concatenate(mixed, axis=0)).astype(BF16)
    o_ref[...] = x + jnp.dot(vb_scr[...], wo_ref[...].astype(BF16), preferred_element_type=F32)

    @pl.when(i == pl.num_programs(0) - 1)
    def _():
        vs_ref[...] = v_scr[tm - vs_ref.shape[0]:, :]


def _gmlp(x, gains, w_in, b_in, vg, ws, bs, wo, layer, j, n_sample, n_prompt_chunks):
    t, d = x.shape
    ng = ws.shape[2]
    d_gmlp = wo.shape[1]
    tm = TOKEN_TILE
    kern = functools.partial(_gmlp_kernel, n_prompt_chunks=n_prompt_chunks)
    return pl.pallas_call(
        kern,
        grid=(t // tm,),
        in_specs=[
            pl.BlockSpec((tm, d), lambda i: (i, 0)),
            pl.BlockSpec((None, 1, d), lambda i: (3 * layer + 1, 0, 0)),
            _resident((None, d, 2 * d_gmlp), lambda i: (j, 0, 0)),
            _resident((None, 1, 2 * d_gmlp), lambda i: (j, 0, 0)),
            _resident((None, 1, d_gmlp), lambda i: (j, 0, 0)),
            _resident((None, 2, ng, CHUNK, CHUNK), lambda i: (j, 0, 0, 0, 0)),
            _resident((None, 2, ng, CHUNK, 1), lambda i: (j, 0, 0, 0, 0)),
            _resident((None, d_gmlp, d), lambda i: (j, 0, 0)),
        ],
        out_specs=[
            pl.BlockSpec((tm, d), lambda i: (i, 0)),
            pl.BlockSpec((n_sample, d_gmlp), lambda i: (0, 0)),
        ],
        out_shape=[
            jax.ShapeDtypeStruct((t, d), F32),
            jax.ShapeDtypeStruct((n_sample, d_gmlp), F32),
        ],
        scratch_shapes=[
            pltpu.VMEM((tm, d), BF16),
            pltpu.VMEM((tm, d_gmlp), F32),
            pltpu.VMEM((tm, d_gmlp), BF16),
        ],
        compiler_params=_params("arbitrary"),
        name="gmlp_mixer",
    )(x, gains, w_in, b_in, vg, ws, bs, wo)


def _mla_proj_kernel(x_ref, g_ref, cosc_ref, sins_ref, wdq_ref, qg_ref, wuq_ref, wdkv_ref, kvg_ref,
                     wuk_ref, wukt_ref, gr_ref,
                     q_ref, rows_ref, kt_ref, v_ref, kst_ref, qs_ref, *, qk_dim, rope_dim, kst_scale):
    n_heads = wuq_ref.shape[0]
    hb = _rms(x_ref[...], g_ref[...]).astype(BF16)
    cosc = cosc_ref[...]
    sins = sins_ref[...]
    tm = hb.shape[0]

    ckv = jnp.dot(hb, wdkv_ref[...], preferred_element_type=F32)
    c = _rms(ckv[:, :LANE], kvg_ref[...])
    kr = ckv[:, LANE:2 * LANE] * cosc + ckv[:, 2 * LANE:] * sins
    k_nope = jnp.dot(c.astype(BF16), wuk_ref[...], preferred_element_type=F32)
    ssq_r = jnp.sum(kr * kr, axis=-1, keepdims=True)
    lane = lax.broadcasted_iota(jnp.int32, (tm, LANE), 1)
    tail = kr
    for h in range(n_heads):
        kn = k_nope[:, h * LANE:(h + 1) * LANE]
        ssq = jnp.sum(kn * kn, axis=-1, keepdims=True) + ssq_r
        ks = lax.rsqrt(ssq / qk_dim + EPS)
        tail = jnp.where(lane == rope_dim + h, ks, tail)
    rows_ref[:, :LANE] = c
    rows_ref[:, LANE:] = tail[:, :rows_ref.shape[1] - LANE]
    v_ref[...] = c.astype(BF16)
    tail_t = jnp.transpose(tail)
    kt_ref[:LANE, :] = jnp.transpose(c).astype(BF16)
    kt_ref[LANE:, :] = tail_t.astype(BF16)
    for cidx in range(tm // CHUNK):
        kst_ref[cidx] = tail_t[rope_dim:rope_dim + n_heads, cidx * CHUNK:(cidx + 1) * CHUNK] * kst_scale

    cq = _rms(jnp.dot(hb, wdq_ref[...], preferred_element_type=F32), qg_ref[...]).astype(BF16)
    gr = gr_ref[...]
    n_tail = qs_ref.shape[1]
    for h in range(n_heads):
        qh = jnp.dot(cq, wuq_ref[h], preferred_element_type=F32)
        qn = qh[:, :LANE]
        qr = qh[:, LANE:2 * LANE] * cosc + qh[:, 2 * LANE:] * sins
        ssq = jnp.sum(qn * qn + qr * qr, axis=-1, keepdims=True)
        inv = lax.rsqrt(ssq / qk_dim + EPS)
        q_lat = (jnp.dot(qn.astype(BF16), wukt_ref[h], preferred_element_type=F32) * inv).astype(BF16)
        q_rope = ((qr * inv) * gr).astype(BF16)
        q_ref[h, :, :LANE] = q_lat
        q_ref[h, :, LANE:] = q_rope
        qs_ref[h, :, :LANE] = q_lat[tm - n_tail:, :]
        qs_ref[h, :, LANE:] = q_rope[tm - n_tail:, :]


def _mla_proj(x, gains, cosc, sins, w, layer, j, qk_dim, rope_dim, n_sample):
    t, d = x.shape
    tm = TOKEN_TILE
    n_heads = w["wuq"].shape[1]
    q_lora = w["wdq"].shape[2]
    cache_dim = LANE + rope_dim + n_heads
    kern = functools.partial(_mla_proj_kernel, qk_dim=qk_dim, rope_dim=rope_dim,
                             kst_scale=qk_dim ** -0.5 * LOG2_E)
    return pl.pallas_call(
        kern,
        grid=(t // tm,),
        in_specs=[
            pl.BlockSpec((tm, d), lambda i: (i, 0)),
            pl.BlockSpec((None, 1, d), lambda i: (3 * layer + 1, 0, 0)),
            pl.BlockSpec((tm, LANE), lambda i: (i, 0)),
            pl.BlockSpec((tm, LANE), lambda i: (i, 0)),
            _resident((None, d, q_lora), lambda i: (j, 0, 0)),
            _resident((None, 1, q_lora), lambda i: (j, 0, 0)),
            _resident((None, n_heads, q_lora, 3 * LANE), lambda i: (j, 0, 0, 0)),
            _resident((None, d, 3 * LANE), lambda i: (j, 0, 0)),
            _resident((None, 1, LANE), lambda i: (j, 0, 0)),
            _resident((None, LANE, n_heads * LANE), lambda i: (j, 0, 0)),
            _resident((None, n_heads, LANE, LANE), lambda i: (j, 0, 0, 0)),
            _resident((None, 1, LANE), lambda i: (j, 0, 0)),
        ],
        out_specs=[
            pl.BlockSpec((n_heads, tm, 2 * LANE), lambda i: (0, i, 0)),
            pl.BlockSpec((tm, cache_dim), lambda i: (i, 0)),
            pl.BlockSpec((2 * LANE, tm), lambda i: (0, i)),
            pl.BlockSpec((tm, LANE), lambda i: (i, 0)),
            pl.BlockSpec((tm // CHUNK, n_heads, CHUNK), lambda i: (i, 0, 0)),
            pl.BlockSpec((n_heads, n_sample, 2 * LANE), lambda i: (0, 0, 0)),
        ],
        out_shape=[
            jax.ShapeDtypeStruct((n_heads, t, 2 * LANE), BF16),
            jax.ShapeDtypeStruct((t, cache_dim), F32),
            jax.ShapeDtypeStruct((2 * LANE, t), BF16),
            jax.ShapeDtypeStruct((t, LANE), BF16),
            jax.ShapeDtypeStruct((t // CHUNK, n_heads, CHUNK), F32),
            jax.ShapeDtypeStruct((n_heads, n_sample, 2 * LANE), BF16),
        ],
        compiler_params=_params("arbitrary"),
        name="mla_proj",
    )(x, gains, cosc, sins, w["wdq"], w["qg"], w["wuq"], w["wdkv"], w["kvg"], w["wuk"], w["wukt"],
      w["gr"])


def _flash_kernel(q_ref, kt_ref, v_ref, kst_ref, os_ref, o_ref, s_scr, p_scr, m_scr, l_scr, acc_scr, *, nq):
    g = pl.program_id(0)
    n_prompt_blocks = pl.num_programs(0) - 1

    @pl.when(g < n_prompt_blocks)
    def _():
        _flash_block(g % nq, q_ref, kt_ref, v_ref, kst_ref, o_ref, s_scr, p_scr, m_scr, l_scr, acc_scr)

    @pl.when(g == n_prompt_blocks)
    def _():
        o_ref[...] = os_ref[...].astype(BF16)


def _flash_block(qi, q_ref, kt_ref, v_ref, kst_ref, o_ref, s_scr, p_scr, m_scr, l_scr, acc_scr):
    n_heads, bq, dk = q_ref.shape
    kb = s_scr.shape[1]
    n_sub = kb // CHUNK
    m_scr[...] = jnp.full_like(m_scr, NEG_INF)
    l_scr[...] = jnp.zeros_like(l_scr)
    acc_scr[...] = jnp.zeros_like(acc_scr)

    def block(j, width=kb, masked=False):
        start = pl.multiple_of(j * kb, kb)
        w_sub = width // CHUNK
        kblk = kt_ref[:, pl.ds(start, width)]
        vblk = v_ref[pl.ds(start, width), :]
        vext = jnp.concatenate([vblk, jnp.ones_like(vblk)], axis=1)
        if masked:
            kpos = start + lax.broadcasted_iota(jnp.int32, (bq, width), 1)
            qpos = qi * bq + lax.broadcasted_iota(jnp.int32, (bq, width), 0)
            visible = kpos <= qpos
        for g0 in range(0, n_heads, FLASH_HEAD_GROUP):
            grp = slice(g0 * bq, (g0 + FLASH_HEAD_GROUP) * bq)
            qg = q_ref[g0:g0 + FLASH_HEAD_GROUP].reshape(FLASH_HEAD_GROUP * bq, dk)
            s_scr[grp, :width] = jnp.dot(qg, kblk, preferred_element_type=F32)
            for h in range(g0, g0 + FLASH_HEAD_GROUP):
                rows = slice(h * bq, (h + 1) * bq)
                ks = jnp.concatenate([kst_ref[j * n_sub + c, h:h + 1, :] for c in range(w_sub)], axis=1)
                s = s_scr[rows, :width] * ks
                if masked:
                    s = jnp.where(visible, s, NEG_INF)
                m_prev = m_scr[rows, :]
                m_new = jnp.maximum(m_prev, jnp.max(s, axis=-1, keepdims=True))
                alpha = jnp.exp2(m_prev - m_new)
                p = jnp.exp2(s - jnp.concatenate([m_new] * w_sub, axis=1))
                m_scr[rows, :] = m_new
                p_scr[rows, :width] = p.astype(BF16)
                l_scr[rows, :] = l_scr[rows, :] * alpha
                acc_scr[rows, :] = acc_scr[rows, :] * alpha
            pv = jnp.dot(p_scr[grp, :width], vext, preferred_element_type=F32)
            acc_scr[grp, :] += pv[:, :CHUNK]
            l_scr[grp, :] += pv[:, CHUNK:]

    last = (qi * bq) // kb

    def body(i, carry):
        block(2 * i)
        block(2 * i + 1)
        return carry

    lax.fori_loop(0, last // 2, body, 0)

    @pl.when(last % 2 == 1)
    def _():
        block(last - 1)

    diag_step = kb // FLASH_DIAG_VARIANTS
    diag_steps = (qi * bq - last * kb) // diag_step + 1
    for r in range(1, FLASH_DIAG_VARIANTS + 1):
        @pl.when(diag_steps == r)
        def _():
            block(last, width=r * diag_step, masked=True)

    for h in range(n_heads):
        rows = slice(h * bq, (h + 1) * bq)
        o_ref[:, h * CHUNK:(h + 1) * CHUNK] = (acc_scr[rows, :] / l_scr[rows, :]).astype(BF16)


def _flash(q, kt, v, kst, o_s, batch, seq):
    n_heads, t_total, dk = q.shape
    bq = CHUNK
    nq = seq // bq
    kb = KV_BLOCK
    assert o_s.shape[0] == bq and t_total == batch * seq + bq

    def kv_block(g):
        return jnp.minimum(g // nq, batch - 1)

    return pl.pallas_call(
        functools.partial(_flash_kernel, nq=nq),
        grid=(batch * nq + 1,),
        in_specs=[
            pl.BlockSpec((n_heads, bq, dk), lambda g: (0, g, 0)),
            pl.BlockSpec((dk, seq), lambda g: (0, kv_block(g))),
            pl.BlockSpec((seq, CHUNK), lambda g: (kv_block(g), 0)),
            pl.BlockSpec((seq // CHUNK, n_heads, CHUNK), lambda g: (kv_block(g), 0, 0)),
            pl.BlockSpec(o_s.shape, lambda g: (0, 0)),
        ],
        out_specs=pl.BlockSpec((bq, n_heads * CHUNK), lambda g: (g, 0)),
        out_shape=jax.ShapeDtypeStruct((t_total, n_heads * CHUNK), BF16),
        scratch_shapes=[
            pltpu.VMEM((n_heads * bq, kb), F32),
            pltpu.VMEM((n_heads * bq, kb), BF16),
            pltpu.VMEM((n_heads * bq, LANE), F32),
            pltpu.VMEM((n_heads * bq, LANE), F32),
            pltpu.VMEM((n_heads * bq, CHUNK), F32),
        ],
        compiler_params=_params("arbitrary"),
        name="flash_prompt",
    )(q, kt, v, kst, o_s)


def _decode_kernel(pt_ref, q_ref, row_ref, cache_ref, o_ref, kbuf, sem, *, layer_slot, rope_dim, sm_scale):
    b = pl.program_id(0)
    nb = pl.num_programs(0)
    n_pages = pt_ref.shape[1]
    cache_dim, page = cache_ref.shape[2:]
    n_group, n_heads, dk = q_ref.shape
    qk_dim = LANE + rope_dim
    slot = b % 2

    def page_copy(pool_page, u, p, s):
        return pltpu.make_async_copy(
            cache_ref.at[layer_slot, pool_page],
            kbuf.at[s, u, pl.ds(0, cache_dim), pl.ds(pl.multiple_of(p * page, page), page)],
            sem.at[s])

    def start_all(step, s):
        for u in range(n_group):
            def go(p, carry):
                page_copy(pt_ref[step * n_group + u, p], u, p, s).start()
                return carry
            lax.fori_loop(0, n_pages, go, 0, unroll=8)

    @pl.when(b == 0)
    def _():
        kbuf[:, :, cache_dim:, :] = jnp.zeros((2, n_group, dk - cache_dim, kbuf.shape[3]), F32)
        start_all(0, 0)

    @pl.when(b + 1 < nb)
    def _():
        start_all(b + 1, 1 - slot)

    for u in range(n_group):
        for p in range(n_pages):
            page_copy(0, u, p, slot).wait()

    for u in range(n_group):
        kt = kbuf[slot, u]
        kt16 = kt.astype(BF16)
        q = q_ref[u]
        s = jnp.dot(q, kt16, preferred_element_type=F32)
        s = (s * kt[qk_dim:qk_dim + n_heads, :]) * sm_scale

        row = row_ref[u]
        qf = q.astype(F32)
        lane = lax.broadcasted_iota(jnp.int32, qf.shape, 1)
        head = lax.broadcasted_iota(jnp.int32, qf.shape, 0)
        ks_new = jnp.sum(jnp.where(lane == qk_dim + head, row, 0.0), axis=-1, keepdims=True)
        s_new = (jnp.sum(qf * row, axis=-1, keepdims=True) * ks_new) * sm_scale

        m = jnp.maximum(jnp.max(s, axis=-1, keepdims=True), s_new)
        p = jnp.exp(s - m)
        p_new = jnp.exp(s_new - m)
        denom = jnp.sum(p, axis=-1, keepdims=True) + p_new
        pv = lax.dot_general(p.astype(BF16), kt16[:LANE, :], (((1,), (1,)), ((), ())),
                             preferred_element_type=F32)
        o_ref[u] = (pv + p_new * row[:, :LANE]) / denom


def _decode(page_table, q_s, rows_s, cache_t, layer_slot, rope_dim, qk_dim):
    n_seq, n_heads, dk = q_s.shape
    past = page_table.shape[1] * cache_t.shape[3]
    grp = DECODE_SEQS
    kern = functools.partial(_decode_kernel, layer_slot=layer_slot, rope_dim=rope_dim,
                             sm_scale=qk_dim ** -0.5)
    grid_spec = pltpu.PrefetchScalarGridSpec(
        num_scalar_prefetch=1,
        grid=(n_seq // grp,),
        in_specs=[
            pl.BlockSpec((grp, n_heads, dk), lambda b, pt: (b, 0, 0)),
            pl.BlockSpec((grp, 1, dk), lambda b, pt: (b, 0, 0)),
            pl.BlockSpec(memory_space=pl.ANY),
        ],
        out_specs=pl.BlockSpec((grp, n_heads, LANE), lambda b, pt: (b, 0, 0)),
        scratch_shapes=[
            pltpu.VMEM((2, grp, dk, past), F32),
            pltpu.SemaphoreType.DMA((2,)),
        ],
    )
    return pl.pallas_call(
        kern,
        grid_spec=grid_spec,
        out_shape=jax.ShapeDtypeStruct((n_seq, n_heads, LANE), F32),
        compiler_params=_params("arbitrary"),
        name="decode_attn",
    )(page_table, q_s, rows_s, cache_t)


def _attn_out_kernel(x_ref, o_ref, wuv_ref, wo_ref, y_ref, ov_scr):
    n_heads = wuv_ref.shape[0]
    for h in range(n_heads):
        cols = slice(h * LANE, (h + 1) * LANE)
        ov_scr[:, cols] = jnp.dot(o_ref[:, cols], wuv_ref[h], preferred_element_type=F32).astype(BF16)
    y_ref[...] = x_ref[...] + jnp.dot(ov_scr[...], wo_ref[...].astype(BF16), preferred_element_type=F32)


def _attn_out(x, o_all, wuv, wo, j):
    t, d = x.shape
    tm = TOKEN_TILE
    n_heads = wuv.shape[1]
    od = o_all.shape[1]
    return pl.pallas_call(
        _attn_out_kernel,
        grid=(t // tm,),
        in_specs=[
            pl.BlockSpec((tm, d), lambda i: (i, 0)),
            pl.BlockSpec((tm, od), lambda i: (i, 0)),
            _resident((None, n_heads, LANE, LANE), lambda i: (j, 0, 0, 0)),
            _resident((None, od, d), lambda i: (j, 0, 0)),
        ],
        out_specs=pl.BlockSpec((tm, d), lambda i: (i, 0)),
        out_shape=jax.ShapeDtypeStruct((t, d), F32),
        scratch_shapes=[pltpu.VMEM((tm, od), BF16)],
        compiler_params=_params("arbitrary"),
        name="attn_out",
    )(x, o_all, wuv, wo)


def _rope_tables(pos, rope_dim, theta=10000.0):
    half = rope_dim // 2
    inv = theta ** (-jnp.arange(half, dtype=F32) / half)
    ang = pos.astype(F32)[:, None] * inv[None, :]
    cos, sin = jnp.cos(ang), jnp.sin(ang)
    zeros = jnp.zeros((pos.shape[0], LANE - rope_dim), F32)
    return (jnp.concatenate([cos, cos, zeros], axis=1),
            jnp.concatenate([-sin, sin, zeros], axis=1))


def _rope_cols(w, nope, half):
    lead = w.shape[:-1]
    pad_n = jnp.zeros(lead + (LANE - nope,), w.dtype)
    pad_r = jnp.zeros(lead + (LANE - 2 * half,), w.dtype)
    r1, r2 = w[..., nope:nope + half], w[..., nope + half:]
    return jnp.concatenate([w[..., :nope], pad_n, r1, r2, pad_r, r2, r1, pad_r], axis=-1)


def kernel(x_prompt, x_sample, cache_mla, page_table, norm_g, ffn_w_gate, ffn_w_up, ffn_w_down,
           gmlp_w_in, gmlp_b_in, gmlp_v_norm, gmlp_w_s, gmlp_b_s, gmlp_w_out,
           mla_w_dq, mla_q_norm, mla_w_uq, mla_w_dkv, mla_kv_norm, mla_w_uk, mla_w_uv,
           mla_qk_gain_q, mla_qk_gain_k, mla_w_o):
    batch, seq, d = x_prompt.shape
    n_sample = x_sample.shape[0]
    depth = norm_g.shape[0]
    n_a, n_b = gmlp_w_in.shape[0], mla_w_dq.shape[0]
    d_ff = ffn_w_gate.shape[3]
    d_gmlp = gmlp_w_out.shape[1]
    n_groups = gmlp_w_s.shape[1]
    gd = d_gmlp // n_groups
    n_heads, qk_dim = mla_w_uq.shape[2], mla_w_uq.shape[3]
    kv_lora = mla_kv_norm.shape[1]
    nope = mla_w_uk.shape[3]
    rope_dim = qk_dim - nope
    half = rope_dim // 2
    past = page_table.shape[1] * cache_mla.shape[2]
    n_prompt = batch * seq
    t = n_prompt + n_sample
    assert x_sample.shape[1] == 1 and n_sample == CHUNK and kv_lora == LANE and nope == LANE
    assert t % TOKEN_TILE == 0 and seq % KV_BLOCK == 0 and d_ff % MXU_N == 0

    wg, wu, wd = ffn_w_gate, ffn_w_up, ffn_w_down
    gains = norm_g.reshape(depth * 3, 1, d)

    g_bin = gmlp_b_in[:, None, :]
    g_vg = gmlp_v_norm[:, None, :]
    tril = jnp.tril(jnp.ones((CHUNK, CHUNK), dtype=bool))
    ws_prompt = jnp.where(tril, gmlp_w_s, 0.0)
    ws_sample = jnp.eye(CHUNK, dtype=F32) * gmlp_w_s[:, :, :1, :1]
    g_ws = jnp.stack([ws_prompt, ws_sample], axis=1).astype(BF16)
    bs_sample = jnp.broadcast_to(gmlp_b_s[:, :, :1], gmlp_b_s.shape)
    g_bs = jnp.stack([gmlp_b_s, bs_sample], axis=1)[..., None]

    qk_gain = mla_qk_gain_q * mla_qk_gain_k
    gr = jnp.concatenate([qk_gain[:, nope:], jnp.zeros((n_b, LANE - rope_dim), F32)], axis=1)[:, None, :]
    mla = {
        "wdq": mla_w_dq.astype(BF16),
        "qg": mla_q_norm[:, None, :],
        "wuq": _rope_cols(mla_w_uq.transpose(0, 2, 1, 3), nope, half).astype(BF16),
        "wdkv": _rope_cols(mla_w_dkv, kv_lora, half).astype(BF16),
        "kvg": mla_kv_norm[:, None, :],
        "wuk": mla_w_uk.reshape(n_b, kv_lora, n_heads * nope).astype(BF16),
        "wukt": (mla_w_uk.transpose(0, 2, 3, 1) * qk_gain[:, None, :nope, None]).astype(BF16),
        "gr": gr,
    }
    m_wuv = mla_w_uv.transpose(0, 2, 1, 3).astype(BF16)

    pos = jnp.concatenate([jnp.tile(jnp.arange(seq, dtype=jnp.int32), batch),
                           jnp.full((n_sample,), past, jnp.int32)])
    cosc, sins = _rope_tables(pos, rope_dim)

    cache_t = jnp.swapaxes(cache_mla, 2, 3)

    x = (x_prompt.reshape(n_prompt, d), x_sample.reshape(n_sample, d))
    new_rows, new_v = [], []
    for i in range(depth):
        x = _ffn(x, gains, wg, wu, wd, i, 0)
        j = i // 2
        if i % 2 == 0:
            x, v_s = _gmlp(x, gains, gmlp_w_in, g_bin, g_vg, g_ws, g_bs, gmlp_w_out, i, j,
                           n_sample, n_prompt // CHUNK)
            new_v.append(v_s)
        else:
            q, rows, kt, v, kst, q_tail = _mla_proj(x, gains, cosc, sins, mla, i, j, qk_dim, rope_dim, n_sample)
            q_s = q_tail.transpose(1, 0, 2)
            rows_s = jnp.pad(rows[n_prompt:], ((0, 0), (0, 2 * LANE - rows.shape[1])))[:, None, :]
            o_s = _decode(page_table, q_s, rows_s, cache_t, j, rope_dim, qk_dim)
            o_all = _flash(q, kt, v, kst, o_s.reshape(n_sample, -1), batch, seq)
            x = _attn_out(x, o_all, m_wuv, mla_w_o, j)
            new_rows.append(rows)
        x = _ffn(x, gains, wg, wu, wd, i, 1, split_out=(i == depth - 1))

    y_prompt, y_sample = x
    rows = jnp.stack(new_rows)
    return (y_prompt.reshape(batch, seq, d),
            y_sample.reshape(n_sample, 1, d),
            rows[:, :n_prompt].reshape(n_b, batch, seq, -1),
            rows[:, n_prompt:].reshape(n_b, n_sample, 1, -1),
            jnp.stack(new_v).reshape(n_a, n_sample, 1, d_gmlp))
```

```python
import functools

import jax
import jax.numpy as jnp
from jax import lax
from jax.experimental import pallas as pl
from jax.experimental.pallas import tpu as pltpu

F32 = jnp.float32
BF16 = jnp.bfloat16
EPS = 1e-6
NEG_INF = -1e30

LANE = 128
MXU_N = 256
CHUNK = 128
TOKEN_TILE = 384
KV_BLOCK = 512
FLASH_DIAG_VARIANTS = 4
FLASH_HEAD_GROUP = 2
DECODE_SEQS = 2
LOG2_E = 1.4426950408889634
VMEM_LIMIT = 56 * 1024 * 1024


def _params(*sem):
    return pltpu.CompilerParams(dimension_semantics=sem, vmem_limit_bytes=VMEM_LIMIT)


def _resident(shape, index_map):
    return pl.BlockSpec(shape, index_map, pipeline_mode=pl.Buffered(1))


def _rms(x, gain):
    return (x * lax.rsqrt(jnp.mean(x * x, axis=-1, keepdims=True) + EPS)) * gain


def _ffn_kernel(*refs, n_in_chunks, split_out, n_prompt_chunks):
    n_x = n_in_chunks + 1 if n_in_chunks else 1
    x_refs, (g_ref, wg_ref, wu_ref, wd_ref) = refs[:n_x], refs[n_x:n_x + 4]
    out_refs, (h_scr, acc_scr) = refs[n_x + 4:-2], refs[-2:]
    i = pl.program_id(0)
    if n_in_chunks:
        xs_ref = x_refs[-1]
        x = jnp.concatenate(
            [jnp.where(i * n_in_chunks + c >= n_prompt_chunks, xs_ref[...], x_refs[c][...])
             for c in range(n_in_chunks)], axis=0)
    else:
        x = x_refs[0][...]
    h_scr[...] = _rms(x, g_ref[...]).astype(BF16)
    acc_scr[...] = jnp.zeros_like(acc_scr)

    for c in range(0, wg_ref.shape[1], MXU_N):
        cols = slice(c, c + MXU_N)
        h = h_scr[...]
        gate = jnp.dot(h, wg_ref[:, cols].astype(BF16), preferred_element_type=F32)
        up = jnp.dot(h, wu_ref[:, cols].astype(BF16), preferred_element_type=F32)
        act = (jax.nn.silu(gate) * up).astype(BF16)
        acc_scr[...] += jnp.dot(act, wd_ref[cols, :].astype(BF16), preferred_element_type=F32)
    y = x + 0.5 * acc_scr[...]
    out_refs[0][...] = y
    if split_out:
        out_refs[1][...] = y[y.shape[0] - out_refs[1].shape[0]:, :]


def _ffn(x, gains, wg, wu, wd, layer, k, split_out=False):
    tm = TOKEN_TILE
    d_ff = wg.shape[3]
    if isinstance(x, tuple):
        xp, xs = x
        n_prompt, d = xp.shape
        n_sample = xs.shape[0]
        assert n_sample == CHUNK and tm % CHUNK == 0
        n_in_chunks = tm // CHUNK
        last_chunk = n_prompt // CHUNK - 1
        x_args = [xp] * n_in_chunks + [xs]
        x_specs = [pl.BlockSpec((CHUNK, d), functools.partial(
            lambda i, c: (jnp.minimum(i * n_in_chunks + c, last_chunk), 0), c=c)) for c in range(n_in_chunks)]
        x_specs.append(pl.BlockSpec((n_sample, d), lambda i: (0, 0)))
        t = n_prompt + n_sample
    else:
        t, d = x.shape
        n_in_chunks = 0
        x_args, x_specs = [x], [pl.BlockSpec((tm, d), lambda i: (i, 0))]
    if split_out:
        n_sample = CHUNK
        n_prompt = t - n_sample
        out_specs = [pl.BlockSpec((tm, d), lambda i: (i, 0)), pl.BlockSpec((n_sample, d), lambda i: (0, 0))]
        out_shape = [jax.ShapeDtypeStruct((n_prompt, d), F32), jax.ShapeDtypeStruct((n_sample, d), F32)]
    else:
        out_specs = pl.BlockSpec((tm, d), lambda i: (i, 0))
        out_shape = jax.ShapeDtypeStruct((t, d), F32)
    kern = functools.partial(_ffn_kernel, n_in_chunks=n_in_chunks, split_out=split_out,
                             n_prompt_chunks=(t - CHUNK) // CHUNK)
    return pl.pallas_call(
        kern,
        grid=(t // tm,),
        in_specs=x_specs + [
            pl.BlockSpec((None, 1, d), lambda i: (3 * layer + 2 * k, 0, 0)),
            _resident((None, None, d, d_ff), lambda i: (layer, k, 0, 0)),
            _resident((None, None, d, d_ff), lambda i: (layer, k, 0, 0)),
            _resident((None, None, d_ff, d), lambda i: (layer, k, 0, 0)),
        ],
        out_specs=out_specs,
        out_shape=out_shape,
        scratch_shapes=[pltpu.VMEM((tm, d), BF16), pltpu.VMEM((tm, d), F32)],
        compiler_params=_params("arbitrary"),
        name="ffn_half",
    )(*x_args, gains, wg, wu, wd)


def _gmlp_kernel(x_ref, g_ref, win_ref, bin_ref, vg_ref, ws_ref, bs_ref, wo_ref,
                 o_ref, vs_ref, h_scr, v_scr, vb_scr, *, n_prompt_chunks):
    i = pl.program_id(0)
    tm = x_ref.shape[0]
    n_groups = ws_ref.shape[1]
    d_gmlp = wo_ref.shape[0]
    gd = d_gmlp // n_groups
    n_sub = tm // CHUNK
    x = x_ref[...]
    h_scr[...] = _rms(x, g_ref[...]).astype(BF16)

    def pre_act(col0):
        cols = slice(col0, col0 + gd)
        z = jnp.dot(h_scr[...], win_ref[:, cols].astype(BF16), preferred_element_type=F32) + bin_ref[:, cols]
        return jax.nn.gelu(z, approximate=True)

    ssq = jnp.zeros((tm, 1), F32)
    for g in range(n_groups):
        v = pre_act(d_gmlp + g * gd)
        v_scr[:, g * gd:(g + 1) * gd] = v
        ssq = ssq + jnp.sum(v * v, axis=-1, keepdims=True)
    inv = lax.rsqrt(ssq / d_gmlp + EPS)
    for g in range(n_groups):
        vn = (v_scr[:, g * gd:(g + 1) * gd] * inv) * vg_ref[:, g * gd:(g + 1) * gd]
        v_scr[:, g * gd:(g + 1) * gd] = vn
        vb_scr[:, g * gd:(g + 1) * gd] = vn.astype(BF16)

    for g in range(n_groups):
        u = pre_act(g * gd)
        mixed = []
        for c in range(n_sub):
            sel = ((i * n_sub + c) >= n_prompt_chunks).astype(jnp.int32)
            vc = vb_scr[c * CHUNK:(c + 1) * CHUNK, g * gd:(g + 1) * gd]
            s = jnp.dot(ws_ref[sel, g], vc, preferred_element_type=F32) + bs_ref[sel, g]
            mixed.append(s)
        vb_scr[:, g * gd:(g + 1) * gd] = (u * jnp.concatenate(mixed, axis=0)).astype(BF16)
    o_ref[...] = x + jnp.dot(vb_scr[...], wo_ref[...].astype(BF16), preferred_element_type=F32)

    @pl.when(i == pl.num_programs(0) - 1)
    def _():
        vs_ref[...] = v_scr[tm - vs_ref.shape[0]:, :]


def _gmlp(x, gains, w_in, b_in, vg, ws, bs, wo, layer, j, n_sample, n_prompt_chunks):
    t, d = x.shape
    ng = ws.shape[2]
    d_gmlp = wo.shape[1]
    tm = TOKEN_TILE
    kern = functools.partial(_gmlp_kernel, n_prompt_chunks=n_prompt_chunks)
    return pl.pallas_call(
        kern,
        grid=(t // tm,),
        in_specs=[
            pl.BlockSpec((tm, d), lambda i: (i, 0)),
            pl.BlockSpec((None, 1, d), lambda i: (3 * layer + 1, 0, 0)),
            _resident((None, d, 2 * d_gmlp), lambda i: (j, 0, 0)),
            _resident((None, 1, 2 * d_gmlp), lambda i: (j, 0, 0)),
            _resident((None, 1, d_gmlp), lambda i: (j, 0, 0)),
            _resident((None, 2, ng, CHUNK, CHUNK), lambda i: (j, 0, 0, 0, 0)),
            _resident((None, 2, ng, CHUNK, 1), lambda i: (j, 0, 0, 0, 0)),
            _resident((None, d_gmlp, d), lambda i: (j, 0, 0)),
        ],
        out_specs=[
            pl.BlockSpec((tm, d), lambda i: (i, 0)),
            pl.BlockSpec((n_sample, d_gmlp), lambda i: (0, 0)),
        ],
        out_shape=[
            jax.ShapeDtypeStruct((t, d), F32),
            jax.ShapeDtypeStruct((n_sample, d_gmlp), F32),
        ],
        scratch_shapes=[
            pltpu.VMEM((tm, d), BF16),
            pltpu.VMEM((tm, d_gmlp), F32),
            pltpu.VMEM((tm, d_gmlp), BF16),
        ],
        compiler_params=_params("arbitrary"),
        name="gmlp_mixer",
    )(x, gains, w_in, b_in, vg, ws, bs, wo)


def _mla_proj_kernel(x_ref, g_ref, cosc_ref, sins_ref, wdq_ref, qg_ref, wuq_ref, wdkv_ref, kvg_ref,
                     wuk_ref, wukt_ref, gr_ref,
                     q_ref, rows_ref, kt_ref, v_ref, kst_ref, qs_ref, *, qk_dim, rope_dim, kst_scale):
    n_heads = wuq_ref.shape[0]
    hb = _rms(x_ref[...], g_ref[...]).astype(BF16)
    cosc = cosc_ref[...]
    sins = sins_ref[...]
    tm = hb.shape[0]

    ckv = jnp.dot(hb, wdkv_ref[...], preferred_element_type=F32)
    c = _rms(ckv[:, :LANE], kvg_ref[...])
    kr = ckv[:, LANE:2 * LANE] * cosc + ckv[:, 2 * LANE:] * sins
    k_nope = jnp.dot(c.astype(BF16), wuk_ref[...], preferred_element_type=F32)
    ssq_r = jnp.sum(kr * kr, axis=-1, keepdims=True)
    lane = lax.broadcasted_iota(jnp.int32, (tm, LANE), 1)
    tail = kr
    for h in range(n_heads):
        kn = k_nope[:, h * LANE:(h + 1) * LANE]
        ssq = jnp.sum(kn * kn, axis=-1, keepdims=True) + ssq_r
        ks = lax.rsqrt(ssq / qk_dim + EPS)
        tail = jnp.where(lane == rope_dim + h, ks, tail)
    rows_ref[:, :LANE] = c
    rows_ref[:, LANE:] = tail[:, :rows_ref.shape[1] - LANE]
    v_ref[...] = c.astype(BF16)
    tail_t = jnp.transpose(tail)
    kt_ref[:LANE, :] = jnp.transpose(c).astype(BF16)
    kt_ref[LANE:, :] = tail_t.astype(BF16)
    for cidx in range(tm // CHUNK):
        kst_ref[cidx] = tail_t[rope_dim:rope_dim + n_heads, cidx * CHUNK:(cidx + 1) * CHUNK] * kst_scale

    cq = _rms(jnp.dot(hb, wdq_ref[...], preferred_element_type=F32), qg_ref[...]).astype(BF16)
    gr = gr_ref[...]
    n_tail = qs_ref.shape[1]
    for h in range(n_heads):
        qh = jnp.dot(cq, wuq_ref[h], preferred_element_type=F32)
        qn = qh[:, :LANE]
        qr = qh[:, LANE:2 * LANE] * cosc + qh[:, 2 * LANE:] * sins
        ssq = jnp.sum(qn * qn + qr * qr, axis=-1, keepdims=True)
        inv = lax.rsqrt(ssq / qk_dim + EPS)
        q_lat = (jnp.dot(qn.astype(BF16), wukt_ref[h], preferred_element_type=F32) * inv).astype(BF16)
        q_rope = ((qr * inv) * gr).astype(BF16)
        q_ref[h, :, :LANE] = q_lat
        q_ref[h, :, LANE:] = q_rope
        qs_ref[h, :, :LANE] = q_lat[tm - n_tail:, :]
        qs_ref[h, :, LANE:] = q_rope[tm - n_tail:, :]


def _mla_proj(x, gains, cosc, sins, w, layer, j, qk_dim, rope_dim, n_sample):
    t, d = x.shape
    tm = TOKEN_TILE
    n_heads = w["wuq"].shape[1]
    q_lora = w["wdq"].shape[2]
    cache_dim = LANE + rope_dim + n_heads
    kern = functools.partial(_mla_proj_kernel, qk_dim=qk_dim, rope_dim=rope_dim,
                             kst_scale=qk_dim ** -0.5 * LOG2_E)
    return pl.pallas_call(
        kern,
        grid=(t // tm,),
        in_specs=[
            pl.BlockSpec((tm, d), lambda i: (i, 0)),
            pl.BlockSpec((None, 1, d), lambda i: (3 * layer + 1, 0, 0)),
            pl.BlockSpec((tm, LANE), lambda i: (i, 0)),
            pl.BlockSpec((tm, LANE), lambda i: (i, 0)),
            _resident((None, d, q_lora), lambda i: (j, 0, 0)),
            _resident((None, 1, q_lora), lambda i: (j, 0, 0)),
            _resident((None, n_heads, q_lora, 3 * LANE), lambda i: (j, 0, 0, 0)),
            _resident((None, d, 3 * LANE), lambda i: (j, 0, 0)),
            _resident((None, 1, LANE), lambda i: (j, 0, 0)),
            _resident((None, LANE, n_heads * LANE), lambda i: (j, 0, 0)),
            _resident((None, n_heads, LANE, LANE), lambda i: (j, 0, 0, 0)),
            _resident((None, 1, LANE), lambda i: (j, 0, 0)),
        ],
        out_specs=[
            pl.BlockSpec((n_heads, tm, 2 * LANE), lambda i: (0, i, 0)),
            pl.BlockSpec((tm, cache_dim), lambda i: (i, 0)),
            pl.BlockSpec((2 * LANE, tm), lambda i: (0, i)),
            pl.BlockSpec((tm, LANE), lambda i: (i, 0)),
            pl.BlockSpec((tm // CHUNK, n_heads, CHUNK), lambda i: (i, 0, 0)),
            pl.BlockSpec((n_heads, n_sample, 2 * LANE), lambda i: (0, 0, 0)),
        ],
        out_shape=[
            jax.ShapeDtypeStruct((n_heads, t, 2 * LANE), BF16),
            jax.ShapeDtypeStruct((t, cache_dim), F32),
            jax.ShapeDtypeStruct((2 * LANE, t), BF16),
            jax.ShapeDtypeStruct((t, LANE), BF16),
            jax.ShapeDtypeStruct((t // CHUNK, n_heads, CHUNK), F32),
            jax.ShapeDtypeStruct((n_heads, n_sample, 2 * LANE), BF16),
        ],
        compiler_params=_params("arbitrary"),
        name="mla_proj",
    )(x, gains, cosc, sins, w["wdq"], w["qg"], w["wuq"], w["wdkv"], w["kvg"], w["wuk"], w["wukt"],
      w["gr"])


def _flash_kernel(q_ref, kt_ref, v_ref, kst_ref, os_ref, o_ref, s_scr, p_scr, m_scr, l_scr, acc_scr, *, nq):
    g = pl.program_id(0)
    n_prompt_blocks = pl.num_programs(0) - 1

    @pl.when(g < n_prompt_blocks)
    def _():
        _flash_block(g % nq, q_ref, kt_ref, v_ref, kst_ref, o_ref, s_scr, p_scr, m_scr, l_scr, acc_scr)

    @pl.when(g == n_prompt_blocks)
    def _():
        o_ref[...] = os_ref[...].astype(BF16)


def _flash_block(qi, q_ref, kt_ref, v_ref, kst_ref, o_ref, s_scr, p_scr, m_scr, l_scr, acc_scr):
    n_heads, bq, dk = q_ref.shape
    kb = s_scr.shape[1]
    n_sub = kb // CHUNK
    m_scr[...] = jnp.full_like(m_scr, NEG_INF)
    l_scr[...] = jnp.zeros_like(l_scr)
    acc_scr[...] = jnp.zeros_like(acc_scr)

    def block(j, width=kb, masked=False):
        start = pl.multiple_of(j * kb, kb)
        w_sub = width // CHUNK
        kblk = kt_ref[:, pl.ds(start, width)]
        vblk = v_ref[pl.ds(start, width), :]
        vext = jnp.concatenate([vblk, jnp.ones_like(vblk)], axis=1)
        if masked:
            kpos = start + lax.broadcasted_iota(jnp.int32, (bq, width), 1)
            qpos = qi * bq + lax.broadcasted_iota(jnp.int32, (bq, width), 0)
            visible = kpos <= qpos
        for g0 in range(0, n_heads, FLASH_HEAD_GROUP):
            grp = slice(g0 * bq, (g0 + FLASH_HEAD_GROUP) * bq)
            qg = q_ref[g0:g0 + FLASH_HEAD_GROUP].reshape(FLASH_HEAD_GROUP * bq, dk)
            s_scr[grp, :width] = jnp.dot(qg, kblk, preferred_element_type=F32)
            for h in range(g0, g0 + FLASH_HEAD_GROUP):
                rows = slice(h * bq, (h + 1) * bq)
                ks = jnp.concatenate([kst_ref[j * n_sub + c, h:h + 1, :] for c in range(w_sub)], axis=1)
                s = s_scr[rows, :width] * ks
                if masked:
                    s = jnp.where(visible, s, NEG_INF)
                m_prev = m_scr[rows, :]
                m_new = jnp.maximum(m_prev, jnp.max(s, axis=-1, keepdims=True))
                alpha = jnp.exp2(m_prev - m_new)
                p = jnp.exp2(s - jnp.concatenate([m_new] * w_sub, axis=1))
                m_scr[rows, :] = m_new
                p_scr[rows, :width] = p.astype(BF16)
                l_scr[rows, :] = l_scr[rows, :] * alpha
                acc_scr[rows, :] = acc_scr[rows, :] * alpha
            pv = jnp.dot(p_scr[grp, :width], vext, preferred_element_type=F32)
            acc_scr[grp, :] += pv[:, :CHUNK]
            l_scr[grp, :] += pv[:, CHUNK:]

    last = (qi * bq) // kb

    def body(i, carry):
        block(2 * i)
        block(2 * i + 1)
        return carry

    lax.fori_loop(0, last // 2, body, 0)

    @pl.when(last % 2 == 1)
    def _():
        block(last - 1)

    diag_step = kb // FLASH_DIAG_VARIANTS
    diag_steps = (qi * bq - last * kb) // diag_step + 1
    for r in range(1, FLASH_DIAG_VARIANTS + 1):
        @pl.when(diag_steps == r)
        def _():
            block(last, width=r * diag_step, masked=True)

    for h in range(n_heads):
        rows = slice(h * bq, (h + 1) * bq)
        o_ref[:, h * CHUNK:(h + 1) * CHUNK] = (acc_scr[rows, :] / l_scr[rows, :]).astype(BF16)


def _flash(q, kt, v, kst, o_s, batch, seq):
    n_heads, t_total, dk = q.shape
    bq = CHUNK
    nq = seq // bq
    kb = KV_BLOCK
    assert o_s.shape[0] == bq and t_total == batch * seq + bq

    def kv_block(g):
        return jnp.minimum(g // nq, batch - 1)

    return pl.pallas_call(
        functools.partial(_flash_kernel, nq=nq),
        grid=(batch * nq + 1,),
        in_specs=[
            pl.BlockSpec((n_heads, bq, dk), lambda g: (0, g, 0)),
            pl.BlockSpec((dk, seq), lambda g: (0, kv_block(g))),
            pl.BlockSpec((seq, CHUNK), lambda g: (kv_block(g), 0)),
            pl.BlockSpec((seq // CHUNK, n_heads, CHUNK), lambda g: (kv_block(g), 0, 0)),
            pl.BlockSpec(o_s.shape, lambda g: (0, 0)),
        ],
        out_specs=pl.BlockSpec((bq, n_heads * CHUNK), lambda g: (g, 0)),
        out_shape=jax.ShapeDtypeStruct((t_total, n_heads * CHUNK), BF16),
        scratch_shapes=[
            pltpu.VMEM((n_heads * bq, kb), F32),
            pltpu.VMEM((n_heads * bq, kb), BF16),
            pltpu.VMEM((n_heads * bq, LANE), F32),
            pltpu.VMEM((n_heads * bq, LANE), F32),
            pltpu.VMEM((n_heads * bq, CHUNK), F32),
        ],
        compiler_params=_params("arbitrary"),
        name="flash_prompt",
    )(q, kt, v, kst, o_s)


def _decode_kernel(pt_ref, q_ref, row_ref, cache_ref, o_ref, kbuf, sem, *, layer_slot, rope_dim, sm_scale):
    b = pl.program_id(0)
    nb = pl.num_programs(0)
    n_pages = pt_ref.shape[1]
    cache_dim, page = cache_ref.shape[2:]
    n_group, n_heads, dk = q_ref.shape
    qk_dim = LANE + rope_dim
    slot = b % 2

    def page_copy(pool_page, u, p, s):
        return pltpu.make_async_copy(
            cache_ref.at[layer_slot, pool_page],
            kbuf.at[s, u, pl.ds(0, cache_dim), pl.ds(pl.multiple_of(p * page, page), page)],
            sem.at[s])

    def start_all(step, s):
        for u in range(n_group):
            def go(p, carry):
                page_copy(pt_ref[step * n_group + u, p], u, p, s).start()
                return carry
            lax.fori_loop(0, n_pages, go, 0, unroll=True)

    @pl.when(b == 0)
    def _():
        kbuf[:, :, cache_dim:, :] = jnp.zeros((2, n_group, dk - cache_dim, kbuf.shape[3]), F32)
        start_all(0, 0)

    @pl.when(b + 1 < nb)
    def _():
        start_all(b + 1, 1 - slot)

    for u in range(n_group):
        for p in range(n_pages):
            page_copy(0, u, p, slot).wait()

    for u in range(n_group):
        kt = kbuf[slot, u]
        kt16 = kt.astype(BF16)
        q = q_ref[u]
        s = jnp.dot(q, kt16, preferred_element_type=F32)
        s = (s * kt[qk_dim:qk_dim + n_heads, :]) * sm_scale

        row = row_ref[u]
        qf = q.astype(F32)
        lane = lax.broadcasted_iota(jnp.int32, qf.shape, 1)
        head = lax.broadcasted_iota(jnp.int32, qf.shape, 0)
        ks_new = jnp.sum(jnp.where(lane == qk_dim + head, row, 0.0), axis=-1, keepdims=True)
        s_new = (jnp.sum(qf * row, axis=-1, keepdims=True) * ks_new) * sm_scale

        m = jnp.maximum(jnp.max(s, axis=-1, keepdims=True), s_new)
        p = jnp.exp(s - m)
        p_new = jnp.exp(s_new - m)
        denom = jnp.sum(p, axis=-1, keepdims=True) + p_new
        pv = lax.dot_general(p.astype(BF16), kt16[:LANE, :], (((1,), (1,)), ((), ())),
                             preferred_element_type=F32)
        o_ref[u] = (pv + p_new * row[:, :LANE]) / denom


def _decode(page_table, q_s, rows_s, cache_t, layer_slot, rope_dim, qk_dim):
    n_seq, n_heads, dk = q_s.shape
    past = page_table.shape[1] * cache_t.shape[3]
    grp = DECODE_SEQS
    kern = functools.partial(_decode_kernel, layer_slot=layer_slot, rope_dim=rope_dim,
                             sm_scale=qk_dim ** -0.5)
    grid_spec = pltpu.PrefetchScalarGridSpec(
        num_scalar_prefetch=1,
        grid=(n_seq // grp,),
        in_specs=[
            pl.BlockSpec((grp, n_heads, dk), lambda b, pt: (b, 0, 0)),
            pl.BlockSpec((grp, 1, dk), lambda b, pt: (b, 0, 0)),
            pl.BlockSpec(memory_space=pl.ANY),
        ],
        out_specs=pl.BlockSpec((grp, n_heads, LANE), lambda b, pt: (b, 0, 0)),
        scratch_shapes=[
            pltpu.VMEM((2, grp, dk, past), F32),
            pltpu.SemaphoreType.DMA((2,)),
        ],
    )
    return pl.pallas_call(
        kern,
        grid_spec=grid_spec,
        out_shape=jax.ShapeDtypeStruct((n_seq, n_heads, LANE), F32),
        compiler_params=_params("arbitrary"),
        name="decode_attn",
    )(page_table, q_s, rows_s, cache_t)


def _attn_out_kernel(x_ref, o_ref, wuv_ref, wo_ref, y_ref, ov_scr):
    n_heads = wuv_ref.shape[0]
    for h in range(n_heads):
        cols = slice(h * LANE, (h + 1) * LANE)
        ov_scr[:, cols] = jnp.dot(o_ref[:, cols], wuv_ref[h], preferred_element_type=F32).astype(BF16)
    y_ref[...] = x_ref[...] + jnp.dot(ov_scr[...], wo_ref[...].astype(BF16), preferred_element_type=F32)


def _attn_out(x, o_all, wuv, wo, j):
    t, d = x.shape
    tm = TOKEN_TILE
    n_heads = wuv.shape[1]
    od = o_all.shape[1]
    return pl.pallas_call(
        _attn_out_kernel,
        grid=(t // tm,),
        in_specs=[
            pl.BlockSpec((tm, d), lambda i: (i, 0)),
            pl.BlockSpec((tm, od), lambda i: (i, 0)),
            _resident((None, n_heads, LANE, LANE), lambda i: (j, 0, 0, 0)),
            _resident((None, od, d), lambda i: (j, 0, 0)),
        ],
        out_specs=pl.BlockSpec((tm, d), lambda i: (i, 0)),
        out_shape=jax.ShapeDtypeStruct((t, d), F32),
        scratch_shapes=[pltpu.VMEM((tm, od), BF16)],
        compiler_params=_params("arbitrary"),
        name="attn_out",
    )(x, o_all, wuv, wo)


def _rope_tables(pos, rope_dim, theta=10000.0):
    half = rope_dim // 2
    inv = theta ** (-jnp.arange(half, dtype=F32) / half)
    ang = pos.astype(F32)[:, None] * inv[None, :]
    cos, sin = jnp.cos(ang), jnp.sin(ang)
    zeros = jnp.zeros((pos.shape[0], LANE - rope_dim), F32)
    return (jnp.concatenate([cos, cos, zeros], axis=1),
            jnp.concatenate([-sin, sin, zeros], axis=1))


def _rope_cols(w, nope, half):
    lead = w.shape[:-1]
    pad_n = jnp.zeros(lead + (LANE - nope,), w.dtype)
    pad_r = jnp.zeros(lead + (LANE - 2 * half,), w.dtype)
    r1, r2 = w[..., nope:nope + half], w[..., nope + half:]
    return jnp.concatenate([w[..., :nope], pad_n, r1, r2, pad_r, r2, r1, pad_r], axis=-1)


def kernel(x_prompt, x_sample, cache_mla, page_table, norm_g, ffn_w_gate, ffn_w_up, ffn_w_down,
           gmlp_w_in, gmlp_b_in, gmlp_v_norm, gmlp_w_s, gmlp_b_s, gmlp_w_out,
           mla_w_dq, mla_q_norm, mla_w_uq, mla_w_dkv, mla_kv_norm, mla_w_uk, mla_w_uv,
           mla_qk_gain_q, mla_qk_gain_k, mla_w_o):
    batch, seq, d = x_prompt.shape
    n_sample = x_sample.shape[0]
    depth = norm_g.shape[0]
    n_a, n_b = gmlp_w_in.shape[0], mla_w_dq.shape[0]
    d_ff = ffn_w_gate.shape[3]
    d_gmlp = gmlp_w_out.shape[1]
    n_groups = gmlp_w_s.shape[1]
    gd = d_gmlp // n_groups
    n_heads, qk_dim = mla_w_uq.shape[2], mla_w_uq.shape[3]
    kv_lora = mla_kv_norm.shape[1]
    nope = mla_w_uk.shape[3]
    rope_dim = qk_dim - nope
    half = rope_dim // 2
    past = page_table.shape[1] * cache_mla.shape[2]
    n_prompt = batch * seq
    t = n_prompt + n_sample
    assert x_sample.shape[1] == 1 and n_sample == CHUNK and kv_lora == LANE and nope == LANE
    assert t % TOKEN_TILE == 0 and seq % KV_BLOCK == 0 and d_ff % MXU_N == 0

    wg, wu, wd = ffn_w_gate, ffn_w_up, ffn_w_down
    gains = norm_g.reshape(depth * 3, 1, d)

    g_bin = gmlp_b_in[:, None, :]
    g_vg = gmlp_v_norm[:, None, :]
    tril = jnp.tril(jnp.ones((CHUNK, CHUNK), dtype=bool))
    ws_prompt = jnp.where(tril, gmlp_w_s, 0.0)
    ws_sample = jnp.eye(CHUNK, dtype=F32) * gmlp_w_s[:, :, :1, :1]
    g_ws = jnp.stack([ws_prompt, ws_sample], axis=1).astype(BF16)
    bs_sample = jnp.broadcast_to(gmlp_b_s[:, :, :1], gmlp_b_s.shape)
    g_bs = jnp.stack([gmlp_b_s, bs_sample], axis=1)[..., None]

    qk_gain = mla_qk_gain_q * mla_qk_gain_k
    gr = jnp.concatenate([qk_gain[:, nope:], jnp.zeros((n_b, LANE - rope_dim), F32)], axis=1)[:, None, :]
    mla = {
        "wdq": mla_w_dq.astype(BF16),
        "qg": mla_q_norm[:, None, :],
        "wuq": _rope_cols(mla_w_uq.transpose(0, 2, 1, 3), nope, half).astype(BF16),
        "wdkv": _rope_cols(mla_w_dkv, kv_lora, half).astype(BF16),
        "kvg": mla_kv_norm[:, None, :],
        "wuk": mla_w_uk.reshape(n_b, kv_lora, n_heads * nope).astype(BF16),
        "wukt": (mla_w_uk.transpose(0, 2, 3, 1) * qk_gain[:, None, :nope, None]).astype(BF16),
        "gr": gr,
    }
    m_wuv = mla_w_uv.transpose(0, 2, 1, 3).astype(BF16)

    cos_p, sin_p = _rope_tables(jnp.arange(seq, dtype=jnp.int32), rope_dim)
    cos_s, sin_s = _rope_tables(jnp.full((1,), past, jnp.int32), rope_dim)
    cosc = jnp.concatenate([jnp.tile(cos_p, (batch, 1)), jnp.tile(cos_s, (n_sample, 1))], axis=0)
    sins = jnp.concatenate([jnp.tile(sin_p, (batch, 1)), jnp.tile(sin_s, (n_sample, 1))], axis=0)

    cache_t = jnp.swapaxes(cache_mla, 2, 3)

    x = (x_prompt.reshape(n_prompt, d), x_sample.reshape(n_sample, d))
    new_rows, new_v = [], []
    for i in range(depth):
        x = _ffn(x, gains, wg, wu, wd, i, 0)
        j = i // 2
        if i % 2 == 0:
            x, v_s = _gmlp(x, gains, gmlp_w_in, g_bin, g_vg, g_ws, g_bs, gmlp_w_out, i, j,
                           n_sample, n_prompt // CHUNK)
            new_v.append(v_s)
        else:
            q, rows, kt, v, kst, q_tail = _mla_proj(x, gains, cosc, sins, mla, i, j, qk_dim, rope_dim, n_sample)
            q_s = q_tail.transpose(1, 0, 2)
            rows_s = jnp.pad(rows[n_prompt:], ((0, 0), (0, 2 * LANE - rows.shape[1])))[:, None, :]
            o_s = _decode(page_table, q_s, rows_s, cache_t, j, rope_dim, qk_dim)
            o_all = _flash(q, kt, v, kst, o_s.reshape(n_sample, -1), batch, seq)
            x = _attn_out(x, o_all, m_wuv, mla_w_o, j)
            new_rows.append(rows)
        x = _ffn(x, gains, wg, wu, wd, i, 1, split_out=(i == depth - 1))

    y_prompt, y_sample = x
    rows = jnp.stack(new_rows)
    return (y_prompt.reshape(batch, seq, d),
            y_sample.reshape(n_sample, 1, d),
            rows[:, :n_prompt].reshape(n_b, batch, seq, -1),
            rows[:, n_prompt:].reshape(n_b, n_sample, 1, -1),
            jnp.stack(new_v).reshape(n_a, n_sample, 1, d_gmlp))
```
